```python
import jax, jax.numpy as jnp
from jax import lax
import numpy as np

D_MODEL = 2048
BATCH = 4
SEQ = 2048
DEPTH = 4
DEC_BATCH = 128
DEC_SEQ = 8
PAST_LEN = 16384
PAGE_SIZE = 128

W_A = D_MODEL // 2
HEAD_A = 64
H_A = W_A // HEAD_A
LORA_W = 64
LORA_A = 64
W_B = D_MODEL // 2
DK_B = 128
H_B = W_B // DK_B
DV_B = W_B // H_B
SHIFT_W = 3 * W_A + LORA_W + LORA_A
OFF_ZA = SHIFT_W
OFF_QB = OFF_ZA + W_A
OFF_FB = OFF_QB + W_B
OFF_IB = OFF_FB + W_B
OFF_ZB = OFF_IB + W_B
OFF_GA = OFF_ZB + W_B
OFF_GB = OFF_GA + D_MODEL
P_TOTAL = OFF_GB + D_MODEL
GLA_CHUNK = 16
NORM_EPS = 1e-6
GN_EPS = 64e-5
MAX_INPUT_GATE = 1.0 - 1e-6

kernel_name = "rwkv7_hgrn2_gated_hybrid_step"


def rms_norm(x, g, eps=NORM_EPS):
    xf = x.astype(jnp.float32)
    y = xf * lax.rsqrt(jnp.mean(xf * xf, axis=-1, keepdims=True) + eps)
    return (y * g.astype(jnp.float32)).astype(x.dtype)


def rwkv7_recurrence(r, w, k, v, kk, a, s0):
    def step(s, inp):
        r_t, w_t, k_t, v_t, kk_t, a_t = inp
        sa = jnp.einsum('nhvk,nhk->nhv', s, kk_t)
        s = (s * w_t[:, :, None, :]
             - sa[..., None] * (kk_t * a_t)[:, :, None, :]
             + v_t[..., None] * k_t[:, :, None, :])
        return s, jnp.einsum('nhvk,nhk->nhv', s, r_t)
    xs = tuple(jnp.swapaxes(t, 0, 1) for t in (r, w, k, v, kk, a))
    s_end, ys = lax.scan(step, s0.astype(jnp.float32), xs)
    return jnp.swapaxes(ys, 0, 1), s_end


def gla_chunkwise(q, k, v, log_g, s0):
    n, h, L, _ = q.shape
    c = min(GLA_CHUNK, L)
    n_chunks = -(-L // c)
    pad = n_chunks * c - L
    if pad:
        cfg = ((0, 0), (0, 0), (0, pad), (0, 0))
        q, k, v, log_g = tuple(jnp.pad(t, cfg) for t in (q, k, v, log_g))

    def to_chunks(t):
        return jnp.moveaxis(t.reshape(n, h, n_chunks, c, t.shape[-1]), 2, 0)

    causal = jnp.tril(jnp.ones((c, c), dtype=bool))[:, :, None]

    def step(s, inp):
        qc, kc, vc, gc = inp
        b = jnp.cumsum(gc, axis=2)
        o = jnp.einsum('nhtd,nhde->nhte', qc * jnp.exp(b), s)
        diff = b[:, :, :, None, :] - b[:, :, None, :, :]
        decay_ts = jnp.where(causal, jnp.exp(jnp.minimum(diff, 0.0)), 0.0)
        att = jnp.einsum('nhtd,nhsd,nhtsd->nhts', qc, kc, decay_ts)
        o = o + jnp.einsum('nhts,nhse->nhte', att, vc)
        b_last = b[:, :, -1:, :]
        s = (jnp.exp(b_last[:, :, 0, :])[..., None] * s
             + jnp.einsum('nhsd,nhse->nhde', kc * jnp.exp(b_last - b), vc))
        return s, o

    s_end, outs = lax.scan(step, s0.astype(jnp.float32),
                           tuple(to_chunks(t) for t in (q, k, v, log_g)))
    o = jnp.moveaxis(outs, 0, 2).reshape(n, h, n_chunks * c, -1)[:, :, :L]
    return o, s_end


def mixer_layer(x, prev_row, s_a, s_b, lb, norm_g, w_in, shift_mu, w0, w2, a0, a2,
                k_k, k_a, r_k, ln_w, ln_b, hg_g, proj_a, proj_b, w_out):
    f32 = jnp.float32
    n, L, _ = x.shape
    h = rms_norm(x, norm_g)
    p = jnp.einsum('nld,dp->nlp', h, w_in)

    cur = p[..., :SHIFT_W]
    prev = jnp.concatenate([prev_row[:, None, :].astype(cur.dtype), cur[:, :-1]], axis=1)
    sh = (cur + (prev - cur) * shift_mu).astype(f32)
    r = sh[..., :W_A]
    k = sh[..., W_A:2 * W_A]
    v = sh[..., 2 * W_A:3 * W_A]
    wd = sh[..., 3 * W_A:3 * W_A + LORA_W]
    ad = sh[..., 3 * W_A + LORA_W:]
    w_log = -jax.nn.softplus(-(w0.astype(f32) + jnp.tanh(wd) @ w2.astype(f32))) - 0.5
    decay = jnp.exp(-jnp.exp(w_log))
    a = jax.nn.sigmoid(a0.astype(f32) + ad @ a2.astype(f32))

    def heads_a(t):
        return t.reshape(n, L, H_A, HEAD_A)

    kk = heads_a(k * k_k.astype(f32))
    kk = kk / jnp.maximum(jnp.sqrt(jnp.sum(kk * kk, axis=-1, keepdims=True)), 1e-12)
    k = k * (1.0 + (a - 1.0) * k_a.astype(f32))
    r_h, k_h, v_h = heads_a(r), heads_a(k), heads_a(v)
    y_a, s_a_new = rwkv7_recurrence(r_h, heads_a(decay), k_h, v_h, kk, heads_a(a), s_a)
    mu = jnp.mean(y_a, axis=-1, keepdims=True)
    var = jnp.mean(jnp.square(y_a - mu), axis=-1, keepdims=True)
    y_a = ((y_a - mu) * lax.rsqrt(var + GN_EPS)).reshape(n, L, W_A) * ln_w.astype(f32) + ln_b.astype(f32)
    bonus = jnp.sum(r_h * k_h * r_k.astype(f32), axis=-1, keepdims=True) * v_h
    y_a = (y_a + bonus.reshape(n, L, W_A)) * jax.nn.silu(p[..., OFF_ZA:OFF_QB].astype(f32))

    def heads_b(t, d):
        return jnp.swapaxes(t.reshape(n, L, H_B, d), 1, 2)

    q_b = jax.nn.silu(p[..., OFF_QB:OFF_FB].astype(f32))
    f_b = p[..., OFF_FB:OFF_IB].astype(f32)
    i_b = p[..., OFF_IB:OFF_ZB].astype(f32)
    k_b = jnp.minimum((1.0 - lb) * jax.nn.sigmoid(-f_b), MAX_INPUT_GATE)
    log_g = jnp.log1p(-k_b)
    o_b, s_b_new = gla_chunkwise(heads_b(q_b, DK_B), heads_b(k_b, DK_B), heads_b(i_b, DV_B),
                                 heads_b(log_g, DK_B), s_b)
    o_b = rms_norm(jnp.swapaxes(o_b, 1, 2), hg_g).reshape(n, L, W_B)
    y_b = o_b * jax.nn.silu(p[..., OFF_ZB:OFF_GA].astype(f32))

    dt = x.dtype
    merged = (jax.nn.sigmoid(p[..., OFF_GA:OFF_GB].astype(f32))
              * jnp.einsum('nlw,wd->nld', y_a.astype(dt), proj_a)
              + jax.nn.sigmoid(p[..., OFF_GB:].astype(f32))
              * jnp.einsum('nlw,wd->nld', y_b.astype(dt), proj_b))
    y = x + jnp.einsum('nld,de->nle', merged.astype(dt), w_out)
    return y, cur[:, -1], s_a_new.astype(s_a.dtype), s_b_new.astype(s_b.dtype)


def setup_inputs(seed: int = 0) -> dict:
    key = jax.random.key(seed)
    ks = jax.random.split(key, 24)
    f32 = jnp.float32

    def nrm(k, shape, s):
        return jax.random.normal(k, shape, f32) * s

    return {
        "x_prompt": nrm(ks[0], (BATCH, SEQ, D_MODEL), 1.0),
        "x_sample": nrm(ks[1], (DEC_BATCH, DEC_SEQ, D_MODEL), 1.0),
        "state_rwkv": nrm(ks[2], (DEPTH, DEC_BATCH, H_A, HEAD_A, HEAD_A), 0.3),
        "state_hgrn": nrm(ks[3], (DEPTH, DEC_BATCH, H_B, DK_B, DV_B), 0.3),
        "state_shift": nrm(ks[4], (DEPTH, DEC_BATCH, SHIFT_W), 1.0),
        "norm_g": 1.0 + nrm(ks[5], (DEPTH, D_MODEL), 0.05),
        "w_in": nrm(ks[6], (DEPTH, D_MODEL, P_TOTAL), D_MODEL ** -0.5),
        "shift_mu": jax.random.uniform(ks[7], (DEPTH, SHIFT_W), f32),
        "rwkv_w0": -0.5 + nrm(ks[8], (DEPTH, W_A), 0.5),
        "rwkv_w2": nrm(ks[9], (DEPTH, LORA_W, W_A), 0.5 * LORA_W ** -0.5),
        "rwkv_a0": nrm(ks[10], (DEPTH, W_A), 0.1),
        "rwkv_a2": nrm(ks[11], (DEPTH, LORA_A, W_A), 0.5 * LORA_A ** -0.5),
        "rwkv_k_k": 0.85 + nrm(ks[12], (DEPTH, W_A), 0.05),
        "rwkv_k_a": 1.0 + nrm(ks[13], (DEPTH, W_A), 0.05),
        "rwkv_r_k": nrm(ks[14], (DEPTH, H_A, HEAD_A), 0.1),
        "rwkv_ln_w": 1.0 + nrm(ks[15], (DEPTH, W_A), 0.05),
        "rwkv_ln_b": nrm(ks[16], (DEPTH, W_A), 0.01),
        "hgrn_lb_logits": nrm(ks[17], (DEPTH, W_B), 0.1),
        "hgrn_norm_g": 1.0 + nrm(ks[18], (DEPTH, DV_B), 0.05),
        "proj_a": nrm(ks[19], (DEPTH, W_A, D_MODEL), W_A ** -0.5),
        "proj_b": nrm(ks[20], (DEPTH, W_B, D_MODEL), W_B ** -0.5),
        "w_out": nrm(ks[21], (DEPTH, D_MODEL, D_MODEL), D_MODEL ** -0.5),
        "final_norm_g": 1.0 + nrm(ks[22], (D_MODEL,), 0.05),
    }


def reference(x_prompt, x_sample, state_rwkv, state_hgrn, state_shift, norm_g, w_in, shift_mu,
              rwkv_w0, rwkv_w2, rwkv_a0, rwkv_a2, rwkv_k_k, rwkv_k_a, rwkv_r_k, rwkv_ln_w,
              rwkv_ln_b, hgrn_lb_logits, hgrn_norm_g, proj_a, proj_b, w_out, final_norm_g):
    probs = jax.nn.softmax(hgrn_lb_logits.astype(jnp.float32), axis=0)
    lbs = jnp.cumsum(probs, axis=0) - probs[0]
    nb = x_prompt.shape[0]
    zero_shift = jnp.zeros((nb, SHIFT_W), x_prompt.dtype)
    zero_a = jnp.zeros((nb, H_A, HEAD_A, HEAD_A), state_rwkv.dtype)
    zero_b = jnp.zeros((nb, H_B, DK_B, DV_B), state_hgrn.dtype)
    xp, xs = x_prompt, x_sample
    p_ra, p_hb, p_sh, s_ra, s_hb, s_sh = [], [], [], [], [], []
    for l in range(DEPTH):
        lw = (norm_g[l], w_in[l], shift_mu[l], rwkv_w0[l], rwkv_w2[l], rwkv_a0[l], rwkv_a2[l],
              rwkv_k_k[l], rwkv_k_a[l], rwkv_r_k[l], rwkv_ln_w[l], rwkv_ln_b[l], hgrn_norm_g[l],
              proj_a[l], proj_b[l], w_out[l])
        lw_head, lw_tail = lw[:12], lw[12:]
        xp, sh_p, a_p, b_p = mixer_layer(xp, zero_shift, zero_a, zero_b, lbs[l], *lw_head, *lw_tail)
        xs, sh_s, a_s, b_s = mixer_layer(xs, state_shift[l], state_rwkv[l], state_hgrn[l], lbs[l],
                                         *lw_head, *lw_tail)
        p_ra.append(a_p); p_hb.append(b_p); p_sh.append(sh_p)
        s_ra.append(a_s); s_hb.append(b_s); s_sh.append(sh_s)
    y_prompt = rms_norm(xp, final_norm_g)
    y_sample = rms_norm(xs, final_norm_g)
    return (y_prompt, y_sample, jnp.stack(p_ra), jnp.stack(p_hb), jnp.stack(p_sh),
            jnp.stack(s_ra), jnp.stack(s_hb), jnp.stack(s_sh))
```

```python
import functools

import jax
import jax.numpy as jnp
from jax import lax
from jax.experimental import pallas as pl
from jax.experimental.pallas import tpu as pltpu

F32 = jnp.float32
BF16 = jnp.bfloat16

NORM_EPS = 1e-6
GN_EPS = 64e-5
MAX_INPUT_GATE = 1.0 - 1e-6
KK_NORM_FLOOR = 1e-12

HEAD_A = 64
HEAD_B = 128
LORA = 64
V7X_LANES = 128
V7X_VMEM_BYTES = 64 * 1024 * 1024
VMEM_LIMIT = (V7X_VMEM_BYTES * 7) // 8
GLA_SUB = 16


def _tile(total, pref, mult=8):
    best = None
    for t in range(mult, min(total, pref) + 1, mult):
        if total % t == 0:
            best = t
    assert best is not None, (total, pref, mult)
    return best


def _params(*sem):
    return pltpu.CompilerParams(dimension_semantics=sem, vmem_limit_bytes=VMEM_LIMIT)


def _sigmoid(x):
    return 1.0 / (1.0 + jnp.exp(-x))


def _split3(x):
    h1 = x.astype(BF16)
    r1 = x - h1.astype(F32)
    h2 = r1.astype(BF16)
    h3 = (r1 - h2.astype(F32)).astype(BF16)
    return h1, h2, h3


def _seg_sum(x, ones_bd):
    outs = []
    for g in range(x.shape[1] // V7X_LANES):
        h1, h2, h3 = _split3(x[:, g * V7X_LANES:(g + 1) * V7X_LANES])
        acc = jnp.dot(h3, ones_bd, preferred_element_type=F32)
        acc = acc + jnp.dot(h2, ones_bd, preferred_element_type=F32)
        acc = acc + jnp.dot(h1, ones_bd, preferred_element_type=F32)
        outs.append(acc)
    return jnp.concatenate(outs, axis=1)


def _lower_bound_body(logits_ref, lb_ref):
    x = logits_ref[...]
    e = jnp.exp(x - jnp.max(x, axis=0, keepdims=True))
    probs = e / jnp.sum(e, axis=0, keepdims=True)
    acc = jnp.zeros_like(probs[0:1])
    lb_ref[0:1, :] = acc
    for l in range(1, x.shape[0]):
        acc = acc + probs[l:l + 1]
        lb_ref[l:l + 1, :] = acc


def _lower_bounds(logits):
    return pl.pallas_call(
        _lower_bound_body,
        out_shape=jax.ShapeDtypeStruct(logits.shape, F32),
        name="hgrn_lower_bounds",
    )(logits)


def _in_proj_body(x_ref, g_ref, w_ref, o_ref, h_ref):
    @pl.when(pl.program_id(1) == 0)
    def _():
        x = x_ref[...]
        ms = jnp.mean(x * x, axis=-1, keepdims=True)
        h_ref[...] = (x * lax.rsqrt(ms + NORM_EPS) * g_ref[...]).astype(BF16)

    o_ref[...] = jnp.dot(h_ref[...], w_ref[...], preferred_element_type=F32)


def _in_proj(x, g, w, tm, tn, name):
    t, d = x.shape
    n = w.shape[1]
    return pl.pallas_call(
        _in_proj_body,
        grid=(t // tm, n // tn),
        in_specs=[
            pl.BlockSpec((tm, d), lambda i, j: (i, 0)),
            pl.BlockSpec((1, d), lambda i, j: (0, 0)),
            pl.BlockSpec((d, tn), lambda i, j: (0, j)),
        ],
        out_specs=pl.BlockSpec((tm, tn), lambda i, j: (i, j)),
        out_shape=jax.ShapeDtypeStruct((t, n), F32),
        scratch_shapes=[pltpu.VMEM((tm, d), BF16)],
        compiler_params=_params("parallel", "arbitrary"),
        name=name,
    )(x, g, w)


def _rwkv_pre_body(cur_ref, prev_ref, mu_ref, w0_ref, a0_ref, kk_ref, ka_ref, rk_ref,
                   lora_hi_ref, lora_lo_ref, ones_ref,
                   r_o, w_o, k_o, v_o, kk_o, b_o, bonus_o):
    wa_ = r_o.shape[1]
    cur = cur_ref[...]
    sh = cur + (prev_ref[...] - cur) * mu_ref[...]
    r = sh[:, :wa_]
    k = sh[:, wa_:2 * wa_]
    v = sh[:, 2 * wa_:3 * wa_]
    wa = sh[:, 3 * wa_:3 * wa_ + 2 * LORA]
    lane = lax.broadcasted_iota(jnp.int32, wa.shape, 1)
    t = jnp.where(lane < LORA, jnp.tanh(wa), wa)
    t1 = t.astype(BF16)
    t2 = (t - t1.astype(F32)).astype(BF16)
    bh = lora_hi_ref[...]
    bl = lora_lo_ref[...]
    lora = (jnp.dot(t1, bl, preferred_element_type=F32)
            + jnp.dot(t2, bh, preferred_element_type=F32)
            + jnp.dot(t1, bh, preferred_element_type=F32))
    z = -(w0_ref[...] + lora[:, :wa_])
    softplus = jnp.maximum(z, 0.0) + jnp.log1p(jnp.exp(-jnp.abs(z)))
    decay = jnp.exp(-jnp.exp(-softplus - 0.5))
    a = _sigmoid(a0_ref[...] + lora[:, wa_:])
    ones_bd = ones_ref[...]
    kk = k * kk_ref[...]
    kk = kk / jnp.maximum(jnp.sqrt(_seg_sum(kk * kk, ones_bd)), KK_NORM_FLOOR)
    k2 = k * (1.0 + (a - 1.0) * ka_ref[...])
    r_o[...] = r
    w_o[...] = decay
    k_o[...] = k2
    v_o[...] = v
    kk_o[...] = kk
    b_o[...] = kk * a
    bonus_o[...] = _seg_sum(r * k2 * rk_ref[...], ones_bd) * v


def _rwkv_pre(cur, prev, mu, w0, a0, k_k, k_a, r_k, lora_hi, lora_lo, ones_bd, tm):
    t, sw = cur.shape
    wa_ = w0.shape[1]
    row = lambda i: (i, 0)
    fixed = lambda i: (0, 0)
    vec = pl.BlockSpec((1, wa_), fixed)
    out = jax.ShapeDtypeStruct((t, wa_), F32)
    return pl.pallas_call(
        _rwkv_pre_body,
        grid=(t // tm,),
        in_specs=[
            pl.BlockSpec((tm, sw), row), pl.BlockSpec((tm, sw), row), pl.BlockSpec((1, sw), fixed),
            vec, vec, vec, vec, vec,
            pl.BlockSpec(lora_hi.shape, fixed), pl.BlockSpec(lora_lo.shape, fixed),
            pl.BlockSpec(ones_bd.shape, fixed),
        ],
        out_specs=[pl.BlockSpec((tm, wa_), row)] * 7,
        out_shape=[out] * 7,
        compiler_params=_params("parallel"),
        name="rwkv_pre",
    )(cur, prev, mu, w0, a0, k_k, k_a, r_k, lora_hi, lora_lo, ones_bd)


def _rwkv_rec_body(kk_ref, w_ref, b_ref, k_ref, r_ref, v_ref, s0_ref, y_ref, s_ref):
    n_key = s_ref.shape[0]

    @pl.when(pl.program_id(1) == 0)
    def _():
        s_ref[...] = s0_ref[...]

    def tree(parts):
        while len(parts) > 1:
            parts = [parts[i] + parts[i + 1] for i in range(0, len(parts), 2)]
        return parts[0]

    def step(t, carry):
        acc = [None] * 4
        for k in range(n_key):
            term = s_ref[k] * kk_ref[t, k:k + 1, :]
            acc[k % 4] = term if acc[k % 4] is None else acc[k % 4] + term
        sa = tree(acc)
        vt = v_ref[t]
        acc = [None] * 4
        for k in range(n_key):
            s_new = (s_ref[k] * w_ref[t, k:k + 1, :]
                     + (vt * k_ref[t, k:k + 1, :] - sa * b_ref[t, k:k + 1, :]))
            s_ref[k] = s_new
            term = s_new * r_ref[t, k:k + 1, :]
            acc[k % 4] = term if acc[k % 4] is None else acc[k % 4] + term
        y_ref[t] = tree(acc)
        return carry

    lax.fori_loop(0, v_ref.shape[0], step, 0)


def _rwkv_rec(kk, w, b, k, r, v, s0, tb):
    length, n_key, lanes = kk.shape
    vl = v.shape[1]
    key_spec = pl.BlockSpec((tb, n_key, V7X_LANES), lambda g, j: (j, 0, g))
    val_spec = pl.BlockSpec((tb, vl, V7X_LANES), lambda g, j: (j, 0, g))
    st_spec = pl.BlockSpec((n_key, vl, V7X_LANES), lambda g, j: (0, 0, g))
    return pl.pallas_call(
        _rwkv_rec_body,
        grid=(lanes // V7X_LANES, length // tb),
        in_specs=[key_spec] * 5 + [val_spec, st_spec],
        out_specs=[val_spec, st_spec],
        out_shape=[jax.ShapeDtypeStruct(v.shape, F32), jax.ShapeDtypeStruct(s0.shape, F32)],
        compiler_params=_params("parallel", "arbitrary"),
        name="rwkv_rec",
    )(kk, w, b, k, r, v, s0)


def _cumsum_rows(x):
    n = x.shape[0]
    row = lax.broadcasted_iota(jnp.int32, x.shape, 0)
    sh = 1
    while sh < n:
        x = x + jnp.where(row >= sh, pltpu.roll(x, sh, 0), 0.0)
        sh *= 2
    return x


def _mm(a, b, dims):
    return lax.dot_general(a.astype(BF16), b.astype(BF16), (dims, ((), ())),
                           preferred_element_type=F32)


def _mm_state(a, st, dims):
    a1 = a.astype(BF16)
    a2 = (a - a1.astype(F32)).astype(BF16)
    s1, s2, s3 = _split3(st)
    dn = (dims, ((), ()))
    dot = functools.partial(lax.dot_general, dimension_numbers=dn, preferred_element_type=F32)
    return (dot(a1, s3) + dot(a2, s2)) + (dot(a1, s2) + dot(a2, s1)) + dot(a1, s1)


def _gla_chunk(qp, f, iv, lb, st, sub):
    c = qp.shape[0]
    q = qp * _sigmoid(qp)
    kb = jnp.minimum((1.0 - lb) * _sigmoid(-f), MAX_INPUT_GATE)
    b = _cumsum_rows(jnp.log1p(-kb))
    o_inter = _mm_state(q * jnp.exp(b), st, ((1,), (1,)))
    outs = []
    for i in range(c // sub):
        lo = i * sub
        qi = q[lo:lo + sub]
        bi = b[lo:lo + sub]
        row = lax.broadcasted_iota(jnp.int32, (sub, 1), 0)
        oi = o_inter[lo:lo + sub]
        for s in range(sub):
            bs = b[lo + s:lo + s + 1]
            e = jnp.exp(jnp.minimum(bi - bs, 0.0))
            att = jnp.sum(qi * e * kb[lo + s:lo + s + 1], axis=-1, keepdims=True)
            oi = oi + jnp.where(row >= s, att, 0.0) * iv[lo + s:lo + s + 1]
        if i > 0:
            b_edge = b[lo - 1:lo]
            qt = qi * jnp.exp(bi - b_edge)
            kh = kb[:lo] * jnp.exp(b_edge - b[:lo])
            att = _mm(qt, kh, ((1,), (1,)))
            oi = oi + _mm(att, iv[:lo], ((1,), (0,)))
        outs.append(oi)
    b_last = b[c - 1:c]
    kt = kb * jnp.exp(b_last - b)
    st_new = st * jnp.exp(b_last) + _mm(iv, kt, ((0,), (0,)))
    return jnp.concatenate(outs, axis=0), st_new


def _hgrn_body(q_ref, f_ref, i_ref, lb_ref, s0_ref, o_ref, s_ref, *, n_seq, chunk, sub):
    @pl.when(pl.program_id(1) == 0)
    def _():
        s_ref[...] = s0_ref[...]

    def head(h, carry):
        col = pl.ds(pl.multiple_of(h * HEAD_B, HEAD_B), HEAD_B)
        lb = lb_ref[:, col]
        for sq in range(n_seq):
            rows = pl.ds(sq * chunk, chunk)
            o, st = _gla_chunk(q_ref[rows, col], f_ref[rows, col], i_ref[rows, col], lb,
                               s_ref[sq, h], sub)
            o_ref[rows, col] = o
            s_ref[sq, h] = st
        return carry

    lax.fori_loop(0, q_ref.shape[1] // HEAD_B, head, 0)


def _hgrn(pb, lb, s0, *, row_off, n_seq_total, length, chunk, n_seq, col_q, col_f, col_i):
    wb = lb.shape[1]
    n_heads = wb // HEAD_B
    rows = n_seq * chunk
    n_chunks = length // chunk
    assert n_seq == 1 or n_chunks == 1
    assert row_off % rows == 0 and n_seq_total % n_seq == 0
    base = row_off // rows

    def tok(cb):
        return pl.BlockSpec((rows, wb), lambda g, c: (base + g * n_chunks + c, cb))

    st_spec = pl.BlockSpec((n_seq, n_heads, HEAD_B, HEAD_B), lambda g, c: (g, 0, 0, 0))
    return pl.pallas_call(
        functools.partial(_hgrn_body, n_seq=n_seq, chunk=chunk, sub=min(GLA_SUB, chunk)),
        grid=(n_seq_total // n_seq, n_chunks),
        in_specs=[tok(col_q), tok(col_f), tok(col_i), pl.BlockSpec((1, wb), lambda g, c: (0, 0)), st_spec],
        out_specs=[pl.BlockSpec((rows, wb), lambda g, c: (g * n_chunks + c, 0)), st_spec],
        out_shape=[jax.ShapeDtypeStruct((n_seq_total * length, wb), F32),
                   jax.ShapeDtypeStruct(s0.shape, F32)],
        compiler_params=_params("parallel", "arbitrary"),
        name="hgrn",
    )(pb, pb, pb, lb, s0)


def _out_proj_body(x_ref, yrec_ref, bonus_ref, za_ref, ob_ref, zb_ref, ga_ref, gb_ref,
                   lnw_ref, lnb_ref, hg_ref, ones_ref, pa_ref, pb_ref, wo_ref, y_ref):
    ones_bd = ones_ref[...]
    inv_head = 1.0 / HEAD_A
    y = yrec_ref[...]
    mu = _seg_sum(y, ones_bd) * inv_head
    yc = y - mu
    var = _seg_sum(yc * yc, ones_bd) * inv_head
    za = za_ref[...]
    ya = (yc * lax.rsqrt(var + GN_EPS) * lnw_ref[...] + lnb_ref[...] + bonus_ref[...]) * (za * _sigmoid(za))
    ob = ob_ref[...]
    zb = zb_ref[...]
    hg = hg_ref[...]
    parts = []
    for g in range(ob.shape[1] // HEAD_B):
        og = ob[:, g * HEAD_B:(g + 1) * HEAD_B]
        ms = jnp.mean(og * og, axis=-1, keepdims=True)
        parts.append(og * lax.rsqrt(ms + NORM_EPS) * hg)
    yb = jnp.concatenate(parts, axis=1) * (zb * _sigmoid(zb))
    merged = (_sigmoid(ga_ref[...]) * jnp.dot(ya.astype(BF16), pa_ref[...], preferred_element_type=F32)
              + _sigmoid(gb_ref[...]) * jnp.dot(yb.astype(BF16), pb_ref[...], preferred_element_type=F32))
    y_ref[...] = x_ref[...] + jnp.dot(merged.astype(BF16), wo_ref[...], preferred_element_type=F32)


def _out_proj(x, yrec, bonus, pb, ob, ln_w, ln_b, hg_g, ones_bd, proj_a, proj_b, w_out,
              *, tm, col_za, col_zb, col_ga, col_gb):
    t, d = x.shape
    wa_ = yrec.shape[1]
    row = lambda i: (i, 0)
    fixed = lambda i: (0, 0)
    half = lambda cb: pl.BlockSpec((tm, wa_), lambda i: (i, cb))
    full = lambda cb: pl.BlockSpec((tm, d), lambda i: (i, cb))
    const = lambda a: pl.BlockSpec(a.shape, fixed, pipeline_mode=pl.Buffered(1))
    return pl.pallas_call(
        _out_proj_body,
        grid=(t // tm,),
        in_specs=[
            pl.BlockSpec((tm, d), row), half(0), half(0), half(col_za), half(0), half(col_zb),
            full(col_ga), full(col_gb),
            const(ln_w), const(ln_b), const(hg_g), const(ones_bd),
            const(proj_a), const(proj_b), const(w_out),
        ],
        out_specs=pl.BlockSpec((tm, d), row),
        out_shape=jax.ShapeDtypeStruct((t, d), F32),
        compiler_params=_params("parallel"),
        name="out_proj",
    )(x, yrec, bonus, pb, ob, pb, pb, pb, ln_w, ln_b, hg_g, ones_bd, proj_a, proj_b, w_out)


def _final_norm_body(x_ref, g_ref, o_ref):
    x = x_ref[...]
    ms = jnp.mean(x * x, axis=-1, keepdims=True)
    o_ref[...] = x * lax.rsqrt(ms + NORM_EPS) * g_ref[...]


def _final_norm(x, g, tm):
    t, d = x.shape
    return pl.pallas_call(
        _final_norm_body,
        grid=(t // tm,),
        in_specs=[pl.BlockSpec((tm, d), lambda i: (i, 0)), pl.BlockSpec((1, d), lambda i: (0, 0))],
        out_specs=pl.BlockSpec((tm, d), lambda i: (i, 0)),
        out_shape=jax.ShapeDtypeStruct((t, d), F32),
        compiler_params=_params("parallel"),
        name="final_norm",
    )(x, g)


def _keys_to_lanes(x, n, length, heads, dup):
    xt = x.reshape(n, length, heads, HEAD_A).transpose(1, 3, 0, 2).reshape(length, HEAD_A, n * heads)
    return jnp.tile(xt, (1, 1, dup)) if dup > 1 else xt


def _vals_to_lanes(x, n, length, heads, dup):
    vl = HEAD_A // dup
    return (x.reshape(n, length, heads, dup, vl).transpose(1, 4, 3, 0, 2)
            .reshape(length, vl, dup * n * heads))


def _vals_from_lanes(y, n, length, heads, dup):
    vl = HEAD_A // dup
    return (y.reshape(length, vl, dup, n, heads).transpose(3, 0, 4, 2, 1)
            .reshape(n * length, heads * HEAD_A))


def _state_to_lanes(s, dup):
    n, heads = s.shape[:2]
    vl = HEAD_A // dup
    return (s.reshape(n, heads, dup, vl, HEAD_A).transpose(4, 3, 2, 0, 1)
            .reshape(HEAD_A, vl, dup * n * heads))


def _state_from_lanes(s, n, heads, dup):
    vl = HEAD_A // dup
    return (s.reshape(HEAD_A, vl, dup, n, heads).transpose(3, 4, 2, 1, 0)
            .reshape(n, heads, HEAD_A, HEAD_A))


def _rwkv_group(streams, s0, n, length, heads, tb):
    kk, w, b, k, r, v = streams
    inst = n * heads
    dup = max(1, V7X_LANES // inst)
    assert (inst * dup) % V7X_LANES == 0 and HEAD_A % dup == 0
    keyed = [_keys_to_lanes(a, n, length, heads, dup) for a in (kk, w, b, k, r)]
    y, s = _rwkv_rec(*keyed, _vals_to_lanes(v, n, length, heads, dup), _state_to_lanes(s0, dup), tb)
    return _vals_from_lanes(y, n, length, heads, dup), _state_from_lanes(s, n, heads, dup)


def kernel(x_prompt, x_sample, state_rwkv, state_hgrn, state_shift, norm_g, w_in, shift_mu, rwkv_w0, rwkv_w2, rwkv_a0, rwkv_a2, rwkv_k_k, rwkv_k_a, rwkv_r_k, rwkv_ln_w, rwkv_ln_b, hgrn_lb_logits, hgrn_norm_g, proj_a, proj_b, w_out, final_norm_g):
    nb, seq, d = x_prompt.shape
    db, dseq, _ = x_sample.shape
    depth = w_in.shape[0]
    wa_ = rwkv_w0.shape[1]
    wb_ = hgrn_lb_logits.shape[1]
    heads_a = wa_ // HEAD_A
    heads_b = wb_ // HEAD_B
    sw = 3 * wa_ + 2 * LORA
    assert w_in.shape[2] == sw + wa_ + 4 * wb_ + 2 * d and wa_ == wb_ and d == 2 * wa_
    tp, ts = nb * seq, db * dseq
    t_all = tp + ts

    off = sw
    za_w = w_in[:, :, off:off + wa_]
    qb_w = w_in[:, :, off + wa_:off + wa_ + wb_]
    fb_w = w_in[:, :, off + wa_ + wb_:off + wa_ + 2 * wb_]
    ib_w = w_in[:, :, off + wa_ + 2 * wb_:off + wa_ + 3 * wb_]
    zb_w = w_in[:, :, off + wa_ + 3 * wb_:off + wa_ + 4 * wb_]
    ga_w = w_in[:, :, off + wa_ + 4 * wb_:off + wa_ + 4 * wb_ + d]
    gb_w = w_in[:, :, off + wa_ + 4 * wb_ + d:]
    w_first = w_in[:, :, :sw].astype(BF16)
    w_second = jnp.concatenate([ga_w, gb_w, za_w, qb_w, fb_w, ib_w, zb_w], axis=2).astype(BF16)
    col_ga, col_gb = 0, 1
    col_za, col_q, col_f, col_i, col_zb = 4, 5, 6, 7, 8
    proj_a_h, proj_b_h, w_out_h = proj_a.astype(BF16), proj_b.astype(BF16), w_out.astype(BF16)

    zero = jnp.zeros((depth, LORA, wa_), F32)
    lora = jnp.concatenate([jnp.concatenate([rwkv_w2, zero], axis=2),
                            jnp.concatenate([zero, rwkv_a2], axis=2)], axis=1)
    lora_hi = lora.astype(BF16)
    lora_lo = (lora - lora_hi.astype(F32)).astype(BF16)
    seg = jnp.arange(V7X_LANES) // HEAD_A
    ones_bd = (seg[:, None] == seg[None, :]).astype(BF16)

    lbs = _lower_bounds(hgrn_lb_logits)
    r_k = rwkv_r_k.reshape(depth, wa_)

    tm_in = _tile(t_all, 1024)
    tm_pre = _tile(t_all, 256)
    tm_out = _tile(t_all, 256)
    tb_p = _tile(seq, 32, 1)
    tb_s = _tile(dseq, 32, 1)
    chunk_p = _tile(seq, 64)
    ns_s = _tile(db, 8, 1)

    x = jnp.concatenate([x_prompt.reshape(tp, d), x_sample.reshape(ts, d)], axis=0)
    zero_shift = jnp.zeros((nb, 1, sw), F32)
    zero_a = jnp.zeros((nb, heads_a, HEAD_A, HEAD_A), F32)
    zero_b = jnp.zeros((nb, heads_b, HEAD_B, HEAD_B), F32)
    outs = [[] for _ in range(6)]
    for l in range(depth):
        row = lambda a: a[l][None]
        p1 = _in_proj(x, row(norm_g), w_first[l], tm_in, _tile(sw, 640, V7X_LANES), "in_proj_shift")
        p2 = _in_proj(x, row(norm_g), w_second[l], tm_in, _tile(w_second.shape[2], 512, V7X_LANES), "in_proj_rest")

        cur_p = p1[:tp].reshape(nb, seq, sw)
        cur_s = p1[tp:].reshape(db, dseq, sw)
        prev = jnp.concatenate([
            jnp.concatenate([zero_shift, cur_p[:, :-1]], axis=1).reshape(tp, sw),
            jnp.concatenate([state_shift[l][:, None], cur_s[:, :-1]], axis=1).reshape(ts, sw)], axis=0)
        r, w, k, v, kk, b, bonus = _rwkv_pre(
            p1, prev, row(shift_mu), row(rwkv_w0), row(rwkv_a0), row(rwkv_k_k), row(rwkv_k_a), row(r_k),
            lora_hi[l], lora_lo[l], ones_bd, tm_pre)

        streams = (kk, w, b, k, r, v)
        y_p, sa_p = _rwkv_group([a[:tp] for a in streams], zero_a, nb, seq, heads_a, tb_p)
        y_s, sa_s = _rwkv_group([a[tp:] for a in streams], state_rwkv[l], db, dseq, heads_a, tb_s)
        yrec = jnp.concatenate([y_p, y_s], axis=0)

        cols = dict(col_q=col_q, col_f=col_f, col_i=col_i)
        o_p, sb_p = _hgrn(p2, row(lbs), zero_b, row_off=0, n_seq_total=nb, length=seq,
                          chunk=chunk_p, n_seq=1, **cols)
        o_s, sb_s = _hgrn(p2, row(lbs), jnp.swapaxes(state_hgrn[l], -1, -2), row_off=tp, n_seq_total=db,
                          length=dseq, chunk=dseq, n_seq=ns_s, **cols)
        ob = jnp.concatenate([o_p, o_s], axis=0)

        x = _out_proj(x, yrec, bonus, p2, ob, row(rwkv_ln_w), row(rwkv_ln_b), row(hgrn_norm_g), ones_bd,
                      proj_a_h[l], proj_b_h[l], w_out_h[l],
                      tm=tm_out, col_za=col_za, col_zb=col_zb, col_ga=col_ga, col_gb=col_gb)

        for dst, val in zip(outs, (sa_p, jnp.swapaxes(sb_p, -1, -2), cur_p[:, -1],
                                   sa_s, jnp.swapaxes(sb_s, -1, -2), cur_s[:, -1])):
            dst.append(val)

    y = _final_norm(x, final_norm_g[None], _tile(t_all, 512))
    return (y[:tp].reshape(nb, seq, d), y[tp:].reshape(db, dseq, d)) + tuple(jnp.stack(o) for o in outs)
```

```python
import functools

import jax
import jax.numpy as jnp
from jax import lax
from jax.experimental import pallas as pl
from jax.experimental.pallas import tpu as pltpu

F32 = jnp.float32
BF16 = jnp.bfloat16

NORM_EPS = 1e-6
GN_EPS = 64e-5
MAX_INPUT_GATE = 1.0 - 1e-6
KK_NORM_FLOOR = 1e-12

HEAD_A = 64
HEAD_B = 128
LORA = 64
V7X_LANES = 128
V7X_VMEM_BYTES = 64 * 1024 * 1024
VMEM_LIMIT = (V7X_VMEM_BYTES * 7) // 8
GLA_SUB = 16
T_BLOCK = V7X_LANES


def _tile(total, pref, mult=8):
    best = None
    for t in range(mult, min(total, pref) + 1, mult):
        if total % t == 0:
            best = t
    assert best is not None, (total, pref, mult)
    return best


def _params(*sem):
    return pltpu.CompilerParams(dimension_semantics=sem, vmem_limit_bytes=VMEM_LIMIT)


def _sigmoid(x):
    return 1.0 / (1.0 + jnp.exp(-x))


def _split3(x):
    h1 = x.astype(BF16)
    r1 = x - h1.astype(F32)
    h2 = r1.astype(BF16)
    h3 = (r1 - h2.astype(F32)).astype(BF16)
    return h1, h2, h3


def _head_sum(x, pat):
    groups = x.shape[1] // V7X_LANES
    acc = None
    for g in range(groups):
        for part in reversed(_split3(x[:, g * V7X_LANES:(g + 1) * V7X_LANES])):
            d = jnp.dot(part, pat, preferred_element_type=F32)
            acc = d if acc is None else acc + d
    return jnp.concatenate([acc] * groups, axis=1)


def _lower_bound_body(logits_ref, lb_ref):
    x = logits_ref[...]
    e = jnp.exp(x - jnp.max(x, axis=0, keepdims=True))
    probs = e / jnp.sum(e, axis=0, keepdims=True)
    acc = jnp.zeros_like(probs[0:1])
    lb_ref[0:1, :] = acc
    for l in range(1, x.shape[0]):
        acc = acc + probs[l:l + 1]
        lb_ref[l:l + 1, :] = acc


def _lower_bounds(logits):
    return pl.pallas_call(
        _lower_bound_body,
        out_shape=jax.ShapeDtypeStruct(logits.shape, F32),
        name="hgrn_lower_bounds",
    )(logits)


def _in_proj_body(x_ref, g_ref, w_ref, o_ref, h_ref):
    @pl.when(pl.program_id(1) == 0)
    def _():
        x = x_ref[...]
        ms = jnp.mean(x * x, axis=-1, keepdims=True)
        h_ref[...] = (x * lax.rsqrt(ms + NORM_EPS) * g_ref[...]).astype(BF16)

    o_ref[...] = jnp.dot(h_ref[...], w_ref[...], preferred_element_type=F32)


def _in_proj(x, g, w, tm, tn, name):
    t, d = x.shape
    n = w.shape[1]
    return pl.pallas_call(
        _in_proj_body,
        grid=(t // tm, n // tn),
        in_specs=[
            pl.BlockSpec((tm, d), lambda i, j: (i, 0)),
            pl.BlockSpec((1, d), lambda i, j: (0, 0)),
            pl.BlockSpec((d, tn), lambda i, j: (0, j)),
        ],
        out_specs=pl.BlockSpec((tm, tn), lambda i, j: (i, j)),
        out_shape=jax.ShapeDtypeStruct((t, n), F32),
        scratch_shapes=[pltpu.VMEM((tm, d), BF16)],
        compiler_params=_params("parallel", "arbitrary"),
        name=name,
    )(x, g, w)


def _rwkv_pre_body(cur_ref, prev_ref, mu_ref, w0_ref, a0_ref, kk_ref, ka_ref, rk_ref,
                   lora_hi_ref, lora_lo_ref, pat_ref,
                   r_o, w_o, k_o, v_o, kk_o, b_o, bonus_o):
    wa_ = r_o.shape[1]
    cur = cur_ref[...]
    sh = cur + (prev_ref[...] - cur) * mu_ref[...]
    r = sh[:, :wa_]
    k = sh[:, wa_:2 * wa_]
    v = sh[:, 2 * wa_:3 * wa_]
    wa = sh[:, 3 * wa_:3 * wa_ + 2 * LORA]
    lane = lax.broadcasted_iota(jnp.int32, wa.shape, 1)
    t = jnp.where(lane < LORA, jnp.tanh(wa), wa)
    t1 = t.astype(BF16)
    t2 = (t - t1.astype(F32)).astype(BF16)
    bh = lora_hi_ref[...]
    bl = lora_lo_ref[...]
    lora = (jnp.dot(t1, bl, preferred_element_type=F32)
            + jnp.dot(t2, bh, preferred_element_type=F32)
            + jnp.dot(t1, bh, preferred_element_type=F32))
    z = -(w0_ref[...] + lora[:, :wa_])
    softplus = jnp.maximum(z, 0.0) + jnp.log1p(jnp.exp(-jnp.abs(z)))
    decay = jnp.exp(-jnp.exp(-softplus - 0.5))
    a = _sigmoid(a0_ref[...] + lora[:, wa_:])
    pat = pat_ref[...]
    kk = k * kk_ref[...]
    kk = kk / jnp.maximum(jnp.sqrt(_head_sum(kk * kk, pat)), KK_NORM_FLOOR)
    k2 = k * (1.0 + (a - 1.0) * ka_ref[...])
    r_o[...] = r
    w_o[...] = decay
    k_o[...] = k2
    v_o[...] = v
    kk_o[...] = kk
    b_o[...] = kk * a
    bonus_o[...] = _head_sum(r * k2 * rk_ref[...], pat) * v


def _rwkv_pre(cur, prev, mu, w0, a0, k_k, k_a, r_k, lora_hi, lora_lo, pat, tm):
    t, sw = cur.shape
    wa_ = w0.shape[1]
    row = lambda i: (i, 0)
    fixed = lambda i: (0, 0)
    vec = pl.BlockSpec((1, wa_), fixed)
    out = jax.ShapeDtypeStruct((t, wa_), F32)
    return pl.pallas_call(
        _rwkv_pre_body,
        grid=(t // tm,),
        in_specs=[
            pl.BlockSpec((tm, sw), row), pl.BlockSpec((tm, sw), row), pl.BlockSpec((1, sw), fixed),
            vec, vec, vec, vec, vec,
            pl.BlockSpec(lora_hi.shape, fixed), pl.BlockSpec(lora_lo.shape, fixed),
            pl.BlockSpec(pat.shape, fixed),
        ],
        out_specs=[pl.BlockSpec((tm, wa_), row)] * 7,
        out_shape=[out] * 7,
        compiler_params=_params("parallel"),
        name="rwkv_pre",
    )(cur, prev, mu, w0, a0, k_k, k_a, r_k, lora_hi, lora_lo, pat)


def _rows_to_lanes(x_refs, z_ref):
    for n, x_ref in enumerate(x_refs):
        for c in range(x_ref.shape[1] // V7X_LANES):
            cols = slice(c * V7X_LANES, (c + 1) * V7X_LANES)
            z_ref[n, cols, :] = x_ref[:, cols].T


def _to_lanes_key_body(*refs, n_seq, heads, dup):
    x_refs, o_ref, z_ref = refs[:n_seq], refs[n_seq], refs[n_seq + 1]
    _rows_to_lanes(x_refs, z_ref)
    for k in range(o_ref.shape[0]):
        rows = slice(k * heads, (k + 1) * heads)
        a = jnp.concatenate([z_ref[n, rows, :] for n in range(n_seq)] * dup, axis=0)
        o_ref[k] = a.T


def _to_lanes_val_body(*refs, n_seq, heads, dup):
    x_refs, o_ref, z_ref = refs[:n_seq], refs[n_seq], refs[n_seq + 1]
    _rows_to_lanes(x_refs, z_ref)
    vl = HEAD_A // dup
    for v in range(vl):
        a = jnp.concatenate([z_ref[n, (v * dup + vh) * heads:(v * dup + vh + 1) * heads, :]
                             for vh in range(dup) for n in range(n_seq)], axis=0)
        o_ref[pl.ds(v, T_BLOCK, stride=vl), :] = a.T


def _from_lanes_val_body(y_ref, o_ref, z_ref, *, n_seq, heads, dup):
    vl = HEAD_A // dup
    for v in range(vl):
        a = y_ref[pl.ds(v, T_BLOCK, stride=vl), :].T
        i = 0
        for vh in range(dup):
            for n in range(n_seq):
                z_ref[n, (v * dup + vh) * heads:(v * dup + vh + 1) * heads, :] = a[i * heads:(i + 1) * heads, :]
                i += 1
    for n in range(n_seq):
        for c in range(o_ref.shape[2] // V7X_LANES):
            cols = slice(c * V7X_LANES, (c + 1) * V7X_LANES)
            o_ref[n, :, cols] = z_ref[n, cols, :].T


def _seq_specs(n_seq, length, width):
    blocks = length // T_BLOCK
    return [pl.BlockSpec((T_BLOCK, width), lambda j, n=n: (n * blocks + j, 0)) for n in range(n_seq)]


def _to_lanes_key(x, n_seq, length, heads, dup):
    width = heads * HEAD_A
    assert n_seq * heads * dup == V7X_LANES and length % T_BLOCK == 0
    return pl.pallas_call(
        functools.partial(_to_lanes_key_body, n_seq=n_seq, heads=heads, dup=dup),
        grid=(length // T_BLOCK,),
        in_specs=_seq_specs(n_seq, length, width),
        out_specs=pl.BlockSpec((HEAD_A, T_BLOCK, V7X_LANES), lambda j: (0, j, 0)),
        out_shape=jax.ShapeDtypeStruct((HEAD_A, length, V7X_LANES), F32),
        scratch_shapes=[pltpu.VMEM((n_seq, width, T_BLOCK), F32)],
        compiler_params=_params("parallel"),
        name="to_lanes_key",
    )(*([x] * n_seq))


def _to_lanes_val(x, n_seq, length, heads, dup):
    width = heads * HEAD_A
    vl = HEAD_A // dup
    assert n_seq * heads * dup == V7X_LANES and length % T_BLOCK == 0
    out = pl.pallas_call(
        functools.partial(_to_lanes_val_body, n_seq=n_seq, heads=heads, dup=dup),
        grid=(length // T_BLOCK,),
        in_specs=_seq_specs(n_seq, length, width),
        out_specs=pl.BlockSpec((T_BLOCK * vl, V7X_LANES), lambda j: (j, 0)),
        out_shape=jax.ShapeDtypeStruct((length * vl, V7X_LANES), F32),
        scratch_shapes=[pltpu.VMEM((n_seq, width, T_BLOCK), F32)],
        compiler_params=_params("parallel"),
        name="to_lanes_val",
    )(*([x] * n_seq))
    return out.reshape(length, vl, V7X_LANES)


def _from_lanes_val(y, n_seq, heads, dup):
    length, vl, _ = y.shape
    width = heads * HEAD_A
    out = pl.pallas_call(
        functools.partial(_from_lanes_val_body, n_seq=n_seq, heads=heads, dup=dup),
        grid=(length // T_BLOCK,),
        in_specs=[pl.BlockSpec((T_BLOCK * vl, V7X_LANES), lambda j: (j, 0))],
        out_specs=pl.BlockSpec((n_seq, T_BLOCK, width), lambda j: (0, j, 0)),
        out_shape=jax.ShapeDtypeStruct((n_seq, length, width), F32),
        scratch_shapes=[pltpu.VMEM((n_seq, width, T_BLOCK), F32)],
        compiler_params=_params("parallel"),
        name="from_lanes_val",
    )(y.reshape(length * vl, V7X_LANES))
    return out.reshape(n_seq * length, width)


def _rwkv_rec_body(kk_ref, w_ref, b_ref, k_ref, r_ref, v_ref, s0_ref, y_ref, s_ref):
    n_key = s_ref.shape[0]

    @pl.when(pl.program_id(1) == 0)
    def _():
        s_ref[...] = s0_ref[...]

    def tree(parts):
        while len(parts) > 1:
            parts = [parts[i] + parts[i + 1] for i in range(0, len(parts), 2)]
        return parts[0]

    def step(t, carry):
        tok = pl.ds(t, 1)
        acc = [None] * 4
        for k in range(n_key):
            term = s_ref[k] * kk_ref[k, tok, :]
            acc[k % 4] = term if acc[k % 4] is None else acc[k % 4] + term
        sa = tree(acc)
        vt = v_ref[t]
        acc = [None] * 4
        for k in range(n_key):
            s_new = (s_ref[k] * w_ref[k, tok, :]
                     + (vt * k_ref[k, tok, :] - sa * b_ref[k, tok, :]))
            s_ref[k] = s_new
            term = s_new * r_ref[k, tok, :]
            acc[k % 4] = term if acc[k % 4] is None else acc[k % 4] + term
        y_ref[t] = tree(acc)
        return carry

    lax.fori_loop(0, v_ref.shape[0], step, 0)


def _rwkv_rec(kk, w, b, k, r, v, s0, tb):
    n_key, length, lanes = kk.shape
    vl = v.shape[1]
    key_spec = pl.BlockSpec((n_key, tb, V7X_LANES), lambda g, j: (0, j, g))
    val_spec = pl.BlockSpec((tb, vl, V7X_LANES), lambda g, j: (j, 0, g))
    st_spec = pl.BlockSpec((n_key, vl, V7X_LANES), lambda g, j: (0, 0, g))
    return pl.pallas_call(
        _rwkv_rec_body,
        grid=(lanes // V7X_LANES, length // tb),
        in_specs=[key_spec] * 5 + [val_spec, st_spec],
        out_specs=[val_spec, st_spec],
        out_shape=[jax.ShapeDtypeStruct(v.shape, F32), jax.ShapeDtypeStruct(s0.shape, F32)],
        compiler_params=_params("parallel", "arbitrary"),
        name="rwkv_rec",
    )(kk, w, b, k, r, v, s0)


def _cumsum_rows(x):
    n = x.shape[0]
    row = lax.broadcasted_iota(jnp.int32, x.shape, 0)
    sh = 1
    while sh < n:
        x = x + jnp.where(row >= sh, pltpu.roll(x, sh, 0), 0.0)
        sh *= 2
    return x


def _mm(a, b, dims):
    return lax.dot_general(a.astype(BF16), b.astype(BF16), (dims, ((), ())),
                           preferred_element_type=F32)


def _mm_state(a, st, dims):
    a1 = a.astype(BF16)
    a2 = (a - a1.astype(F32)).astype(BF16)
    s1, s2, s3 = _split3(st)
    dn = (dims, ((), ()))
    dot = functools.partial(lax.dot_general, dimension_numbers=dn, preferred_element_type=F32)
    return (dot(a1, s3) + dot(a2, s2)) + (dot(a1, s2) + dot(a2, s1)) + dot(a1, s1)


def _gla_chunk(qp, f, iv, lb, st, sub):
    c = qp.shape[0]
    q = qp * _sigmoid(qp)
    kb = jnp.minimum((1.0 - lb) * _sigmoid(-f), MAX_INPUT_GATE)
    b = _cumsum_rows(jnp.log1p(-kb))
    o_inter = _mm_state(q * jnp.exp(b), st, ((1,), (1,)))
    outs = []
    for i in range(c // sub):
        lo = i * sub
        qi = q[lo:lo + sub]
        bi = b[lo:lo + sub]
        row = lax.broadcasted_iota(jnp.int32, (sub, 1), 0)
        oi = o_inter[lo:lo + sub]
        for s in range(sub):
            bs = b[lo + s:lo + s + 1]
            e = jnp.exp(jnp.minimum(bi - bs, 0.0))
            att = jnp.sum(qi * e * kb[lo + s:lo + s + 1], axis=-1, keepdims=True)
            oi = oi + jnp.where(row >= s, att, 0.0) * iv[lo + s:lo + s + 1]
        if i > 0:
            b_edge = b[lo - 1:lo]
            qt = qi * jnp.exp(bi - b_edge)
            kh = kb[:lo] * jnp.exp(b_edge - b[:lo])
            att = _mm(qt, kh, ((1,), (1,)))
            oi = oi + _mm(att, iv[:lo], ((1,), (0,)))
        outs.append(oi)
    b_last = b[c - 1:c]
    kt = kb * jnp.exp(b_last - b)
    st_new = st * jnp.exp(b_last) + _mm(iv, kt, ((0,), (0,)))
    return jnp.concatenate(outs, axis=0), st_new


def _hgrn_body(q_ref, f_ref, i_ref, lb_ref, s0_ref, o_ref, s_ref, *, n_seq, chunk, sub):
    @pl.when(pl.program_id(1) == 0)
    def _():
        s_ref[...] = s0_ref[...]

    def head(h, carry):
        col = pl.ds(pl.multiple_of(h * HEAD_B, HEAD_B), HEAD_B)
        lb = lb_ref[:, col]
        for sq in range(n_seq):
            rows = pl.ds(sq * chunk, chunk)
            o, st = _gla_chunk(q_ref[rows, col], f_ref[rows, col], i_ref[rows, col], lb,
                               s_ref[sq, h], sub)
            o_ref[rows, col] = o
            s_ref[sq, h] = st
        return carry

    lax.fori_loop(0, q_ref.shape[1] // HEAD_B, head, 0)


def _hgrn(pb, lb, s0, *, row_off, n_seq_total, length, chunk, n_seq, col_q, col_f, col_i):
    wb = lb.shape[1]
    n_heads = wb // HEAD_B
    rows = n_seq * chunk
    n_chunks = length // chunk
    assert n_seq == 1 or n_chunks == 1
    assert row_off % rows == 0 and n_seq_total % n_seq == 0
    base = row_off // rows

    def tok(cb):
        return pl.BlockSpec((rows, wb), lambda g, c: (base + g * n_chunks + c, cb))

    st_spec = pl.BlockSpec((n_seq, n_heads, HEAD_B, HEAD_B), lambda g, c: (g, 0, 0, 0))
    return pl.pallas_call(
        functools.partial(_hgrn_body, n_seq=n_seq, chunk=chunk, sub=min(GLA_SUB, chunk)),
        grid=(n_seq_total // n_seq, n_chunks),
        in_specs=[tok(col_q), tok(col_f), tok(col_i), pl.BlockSpec((1, wb), lambda g, c: (0, 0)), st_spec],
        out_specs=[pl.BlockSpec((rows, wb), lambda g, c: (g * n_chunks + c, 0)), st_spec],
        out_shape=[jax.ShapeDtypeStruct((n_seq_total * length, wb), F32),
                   jax.ShapeDtypeStruct(s0.shape, F32)],
        compiler_params=_params("parallel", "arbitrary"),
        name="hgrn",
    )(pb, pb, pb, lb, s0)


def _out_proj_body(x_ref, yrec_ref, bonus_ref, za_ref, ob_ref, zb_ref, ga_ref, gb_ref,
                   lnw_ref, lnb_ref, hg_ref, pat_ref, pa_ref, pb_ref, wo_ref, y_ref):
    pat = pat_ref[...]
    inv_head = 1.0 / HEAD_A
    y = yrec_ref[...]
    mu = _head_sum(y, pat) * inv_head
    yc = y - mu
    var = _head_sum(yc * yc, pat) * inv_head
    za = za_ref[...]
    ya = (yc * lax.rsqrt(var + GN_EPS) * lnw_ref[...] + lnb_ref[...] + bonus_ref[...]) * (za * _sigmoid(za))
    ob = ob_ref[...]
    zb = zb_ref[...]
    hg = hg_ref[...]
    parts = []
    for g in range(ob.shape[1] // HEAD_B):
        og = ob[:, g * HEAD_B:(g + 1) * HEAD_B]
        ms = jnp.mean(og * og, axis=-1, keepdims=True)
        parts.append(og * lax.rsqrt(ms + NORM_EPS) * hg)
    yb = jnp.concatenate(parts, axis=1) * (zb * _sigmoid(zb))
    merged = (_sigmoid(ga_ref[...]) * jnp.dot(ya.astype(BF16), pa_ref[...], preferred_element_type=F32)
              + _sigmoid(gb_ref[...]) * jnp.dot(yb.astype(BF16), pb_ref[...], preferred_element_type=F32))
    y_ref[...] = x_ref[...] + jnp.dot(merged.astype(BF16), wo_ref[...], preferred_element_type=F32)


def _out_proj(x, yrec, bonus, pb, ob, ln_w, ln_b, hg_g, pat, proj_a, proj_b, w_out,
              *, tm, col_za, col_zb, col_ga, col_gb):
    t, d = x.shape
    wa_ = yrec.shape[1]
    row = lambda i: (i, 0)
    fixed = lambda i: (0, 0)
    half = lambda cb: pl.BlockSpec((tm, wa_), lambda i: (i, cb))
    full = lambda cb: pl.BlockSpec((tm, d), lambda i: (i, cb))
    const = lambda a: pl.BlockSpec(a.shape, fixed, pipeline_mode=pl.Buffered(1))
    return pl.pallas_call(
        _out_proj_body,
        grid=(t // tm,),
        in_specs=[
            pl.BlockSpec((tm, d), row), half(0), half(0), half(col_za), half(0), half(col_zb),
            full(col_ga), full(col_gb),
            const(ln_w), const(ln_b), const(hg_g), const(pat),
            const(proj_a), const(proj_b), const(w_out),
        ],
        out_specs=pl.BlockSpec((tm, d), row),
        out_shape=jax.ShapeDtypeStruct((t, d), F32),
        compiler_params=_params("parallel"),
        name="out_proj",
    )(x, yrec, bonus, pb, ob, pb, pb, pb, ln_w, ln_b, hg_g, pat, proj_a, proj_b, w_out)


def _final_norm_body(x_ref, g_ref, o_ref):
    x = x_ref[...]
    ms = jnp.mean(x * x, axis=-1, keepdims=True)
    o_ref[...] = x * lax.rsqrt(ms + NORM_EPS) * g_ref[...]


def _final_norm(x, g, tm):
    t, d = x.shape
    return pl.pallas_call(
        _final_norm_body,
        grid=(t // tm,),
        in_specs=[pl.BlockSpec((tm, d), lambda i: (i, 0)), pl.BlockSpec((1, d), lambda i: (0, 0))],
        out_specs=pl.BlockSpec((tm, d), lambda i: (i, 0)),
        out_shape=jax.ShapeDtypeStruct((t, d), F32),
        compiler_params=_params("parallel"),
        name="final_norm",
    )(x, g)


def _perm_key(a, heads):
    lead = a.shape[:-1]
    return jnp.swapaxes(a.reshape(*lead, heads, HEAD_A), -1, -2).reshape(*lead, heads * HEAD_A)


def _unperm_key(a, heads):
    lead = a.shape[:-1]
    return jnp.swapaxes(a.reshape(*lead, HEAD_A, heads), -1, -2).reshape(*lead, heads * HEAD_A)


def _perm_val(a, heads, dup):
    lead = a.shape[:-1]
    x = a.reshape(*lead, heads, dup, HEAD_A // dup)
    return jnp.moveaxis(x, (-3, -2, -1), (-1, -2, -3)).reshape(*lead, heads * HEAD_A)


def _unperm_val(a, heads, dup):
    lead = a.shape[:-1]
    x = a.reshape(*lead, HEAD_A // dup, dup, heads)
    return jnp.moveaxis(x, (-3, -2, -1), (-1, -2, -3)).reshape(*lead, heads * HEAD_A)


def _perm_shift(a, heads, dup):
    w = heads * HEAD_A
    return jnp.concatenate([_perm_key(a[..., :w], heads), _perm_key(a[..., w:2 * w], heads),
                            _perm_val(a[..., 2 * w:3 * w], heads, dup), a[..., 3 * w:]], axis=-1)


def _unperm_shift(a, heads, dup):
    w = heads * HEAD_A
    return jnp.concatenate([_unperm_key(a[..., :w], heads), _unperm_key(a[..., w:2 * w], heads),
                            _unperm_val(a[..., 2 * w:3 * w], heads, dup), a[..., 3 * w:]], axis=-1)


def kernel(x_prompt, x_sample, state_rwkv, state_hgrn, state_shift, norm_g, w_in, shift_mu, rwkv_w0, rwkv_w2, rwkv_a0, rwkv_a2, rwkv_k_k, rwkv_k_a, rwkv_r_k, rwkv_ln_w, rwkv_ln_b, hgrn_lb_logits, hgrn_norm_g, proj_a, proj_b, w_out, final_norm_g):
    nb, seq, d = x_prompt.shape
    db, dseq, _ = x_sample.shape
    depth = w_in.shape[0]
    wa_ = rwkv_w0.shape[1]
    wb_ = hgrn_lb_logits.shape[1]
    heads_a = wa_ // HEAD_A
    heads_b = wb_ // HEAD_B
    sw = 3 * wa_ + 2 * LORA
    assert w_in.shape[2] == sw + wa_ + 4 * wb_ + 2 * d and wa_ == wb_ and d == 2 * wa_
    dup = V7X_LANES // (nb * heads_a)
    assert nb * heads_a * dup == V7X_LANES and (db * heads_a) % V7X_LANES == 0
    vl = HEAD_A // dup
    tp, ts = nb * seq, db * dseq
    t_all = tp + ts
    pk = functools.partial(_perm_key, heads=heads_a)
    pv = functools.partial(_perm_val, heads=heads_a, dup=dup)

    off = sw
    za_w = pv(w_in[:, :, off:off + wa_])
    qb_w = w_in[:, :, off + wa_:off + wa_ + wb_]
    fb_w = w_in[:, :, off + wa_ + wb_:off + wa_ + 2 * wb_]
    ib_w = w_in[:, :, off + wa_ + 2 * wb_:off + wa_ + 3 * wb_]
    zb_w = w_in[:, :, off + wa_ + 3 * wb_:off + wa_ + 4 * wb_]
    ga_w = w_in[:, :, off + wa_ + 4 * wb_:off + wa_ + 4 * wb_ + d]
    gb_w = w_in[:, :, off + wa_ + 4 * wb_ + d:]
    w_first = _perm_shift(w_in[:, :, :sw], heads_a, dup).astype(BF16)
    w_second = jnp.concatenate([ga_w, gb_w, za_w, qb_w, fb_w, ib_w, zb_w], axis=2).astype(BF16)
    col_ga, col_gb = 0, 1
    col_za, col_q, col_f, col_i, col_zb = 4, 5, 6, 7, 8
    proj_a_h = (proj_a.reshape(depth, heads_a, dup, vl, d).transpose(0, 3, 2, 1, 4)
                .reshape(depth, wa_, d).astype(BF16))
    proj_b_h, w_out_h = proj_b.astype(BF16), w_out.astype(BF16)
    mu_p = _perm_shift(shift_mu, heads_a, dup)
    w0_p, a0_p, kk_p, ka_p = pk(rwkv_w0), pk(rwkv_a0), pk(rwkv_k_k), pk(rwkv_k_a)
    rk_p = pk(rwkv_r_k.reshape(depth, wa_))
    lnw_p, lnb_p = pv(rwkv_ln_w), pv(rwkv_ln_b)

    zero = jnp.zeros((depth, LORA, wa_), F32)
    lora = jnp.concatenate([jnp.concatenate([pk(rwkv_w2), zero], axis=2),
                            jnp.concatenate([zero, pk(rwkv_a2)], axis=2)], axis=1)
    lora_hi = lora.astype(BF16)
    lora_lo = (lora - lora_hi.astype(F32)).astype(BF16)
    lane_head = jnp.arange(V7X_LANES) % heads_a
    pat = (lane_head[:, None] == lane_head[None, :]).astype(BF16)

    lbs = _lower_bounds(hgrn_lb_logits)

    tm_in = _tile(t_all, 1024)
    tm_pre = _tile(t_all, 256)
    tm_out = _tile(t_all, 256)
    tb_p = _tile(seq, 32)
    chunk_p = _tile(seq, 64)
    ns_s = _tile(db, 8, 1)

    x = jnp.concatenate([x_prompt.reshape(tp, d), x_sample.reshape(ts, d)], axis=0)
    zero_shift = jnp.zeros((nb, 1, sw), F32)
    zero_a = jnp.zeros((HEAD_A, vl, V7X_LANES), F32)
    zero_b = jnp.zeros((nb, heads_b, HEAD_B, HEAD_B), F32)
    shift_in = _perm_shift(state_shift, heads_a, dup)
    outs = [[] for _ in range(6)]
    for l in range(depth):
        row = lambda a: a[l][None]
        p1 = _in_proj(x, row(norm_g), w_first[l], tm_in, _tile(sw, 640, V7X_LANES), "in_proj_shift")
        p2 = _in_proj(x, row(norm_g), w_second[l], tm_in, _tile(w_second.shape[2], 512, V7X_LANES), "in_proj_rest")

        cur_p = p1[:tp].reshape(nb, seq, sw)
        cur_s = p1[tp:].reshape(db, dseq, sw)
        prev = jnp.concatenate([
            jnp.concatenate([zero_shift, cur_p[:, :-1]], axis=1).reshape(tp, sw),
            jnp.concatenate([shift_in[l][:, None], cur_s[:, :-1]], axis=1).reshape(ts, sw)], axis=0)
        r, w, k, v, kk, b, bonus = _rwkv_pre(
            p1, prev, row(mu_p), row(w0_p), row(a0_p), row(kk_p), row(ka_p), row(rk_p),
            lora_hi[l], lora_lo[l], pat, tm_pre)

        keyed = [_to_lanes_key(a, nb, seq, heads_a, dup) for a in (kk, w, b, k, r)]
        y_p, sa_p = _rwkv_rec(*keyed, _to_lanes_val(v, nb, seq, heads_a, dup), zero_a, tb_p)
        y_p = _from_lanes_val(y_p, nb, heads_a, dup)
        sa_p = (sa_p.reshape(HEAD_A, vl, dup, nb, heads_a).transpose(3, 4, 2, 1, 0)
                .reshape(nb, heads_a, HEAD_A, HEAD_A))
        lanes_s = db * heads_a
        keyed = [a[tp:].reshape(db, dseq, HEAD_A, heads_a).transpose(2, 1, 0, 3).reshape(HEAD_A, dseq, lanes_s)
                 for a in (kk, w, b, k, r)]
        v_s = (v[tp:].reshape(db, dseq, vl, dup, heads_a).transpose(1, 3, 2, 0, 4)
               .reshape(dseq, HEAD_A, lanes_s))
        s0_s = state_rwkv[l].transpose(3, 2, 0, 1).reshape(HEAD_A, HEAD_A, lanes_s)
        y_s, sa_s = _rwkv_rec(*keyed, v_s, s0_s, dseq)
        y_s = (y_s.reshape(dseq, dup, vl, db, heads_a).transpose(3, 0, 2, 1, 4).reshape(ts, wa_))
        sa_s = sa_s.reshape(HEAD_A, HEAD_A, db, heads_a).transpose(2, 3, 1, 0)
        yrec = jnp.concatenate([y_p, y_s], axis=0)

        cols = dict(col_q=col_q, col_f=col_f, col_i=col_i)
        o_p, sb_p = _hgrn(p2, row(lbs), zero_b, row_off=0, n_seq_total=nb, length=seq,
                          chunk=chunk_p, n_seq=1, **cols)
        o_s, sb_s = _hgrn(p2, row(lbs), jnp.swapaxes(state_hgrn[l], -1, -2), row_off=tp, n_seq_total=db,
                          length=dseq, chunk=dseq, n_seq=ns_s, **cols)
        ob = jnp.concatenate([o_p, o_s], axis=0)

        x = _out_proj(x, yrec, bonus, p2, ob, row(lnw_p), row(lnb_p), row(hgrn_norm_g), pat,
                      proj_a_h[l], proj_b_h[l], w_out_h[l],
                      tm=tm_out, col_za=col_za, col_zb=col_zb, col_ga=col_ga, col_gb=col_gb)

        for dst, val in zip(outs, (sa_p, jnp.swapaxes(sb_p, -1, -2), _unperm_shift(cur_p[:, -1], heads_a, dup),
                                   sa_s, jnp.swapaxes(sb_s, -1, -2), _unperm_shift(cur_s[:, -1], heads_a, dup))):
            dst.append(val)

    y = _final_norm(x, final_norm_g[None], _tile(t_all, 512))
    return (y[:tp].reshape(nb, seq, d), y[tp:].reshape(db, dseq, d)) + tuple(jnp.stack(o) for o in outs)
```

```python
import functools
import math

import jax
import jax.numpy as jnp
from jax import lax
from jax.experimental import pallas as pl
from jax.experimental.pallas import tpu as pltpu

F32 = jnp.float32
BF16 = jnp.bfloat16

NORM_EPS = 1e-6
GN_EPS = 64e-5
MAX_INPUT_GATE = 1.0 - 1e-6
KK_NORM_FLOOR = 1e-12

HEAD_A = 64
HEAD_B = 128
LORA = 64
V7X_LANES = 128
V7X_VMEM_BYTES = 64 * 1024 * 1024
VMEM_LIMIT = (V7X_VMEM_BYTES * 7) // 8
GLA_SUB = 16
ROW_TILE = 8
RWKV_STEPS_PER_ITER = 4
T_BLOCK = V7X_LANES


def _tile(total, pref, mult=8):
    best = None
    for t in range(mult, min(total, pref) + 1, mult):
        if total % t == 0:
            best = t
    assert best is not None, (total, pref, mult)
    return best


def _params(*sem):
    return pltpu.CompilerParams(dimension_semantics=sem, vmem_limit_bytes=VMEM_LIMIT)


def _sigmoid(x):
    return 1.0 / (1.0 + jnp.exp(-x))


def _split3(x):
    h1 = x.astype(BF16)
    r1 = x - h1.astype(F32)
    h2 = r1.astype(BF16)
    h3 = (r1 - h2.astype(F32)).astype(BF16)
    return h1, h2, h3


def _head_sum(x, pat):
    groups = x.shape[1] // V7X_LANES
    acc = None
    for g in range(groups):
        for part in reversed(_split3(x[:, g * V7X_LANES:(g + 1) * V7X_LANES])):
            d = jnp.dot(part, pat, preferred_element_type=F32)
            acc = d if acc is None else acc + d
    return jnp.concatenate([acc] * groups, axis=1)


def _lower_bound_body(logits_ref, lb_ref):
    x = logits_ref[...]
    e = jnp.exp(x - jnp.max(x, axis=0, keepdims=True))
    probs = e / jnp.sum(e, axis=0, keepdims=True)
    acc = jnp.zeros_like(probs[0:1])
    lb_ref[0:1, :] = acc
    for l in range(1, x.shape[0]):
        acc = acc + probs[l:l + 1]
        lb_ref[l:l + 1, :] = acc


def _lower_bounds(logits):
    return pl.pallas_call(
        _lower_bound_body,
        out_shape=jax.ShapeDtypeStruct(logits.shape, F32),
        name="hgrn_lower_bounds",
    )(logits)


def _in_proj_body(x_ref, g_ref, w_ref, o_ref, h_ref):
    @pl.when(pl.program_id(1) == 0)
    def _():
        x = x_ref[...]
        ms = jnp.mean(x * x, axis=-1, keepdims=True)
        h_ref[...] = (x * lax.rsqrt(ms + NORM_EPS) * g_ref[...]).astype(BF16)

    o_ref[...] = jnp.dot(h_ref[...], w_ref[...], preferred_element_type=F32)


def _in_proj(x, g, w, layer, tm, tn, name):
    t, d = x.shape
    n = w.shape[2]
    return pl.pallas_call(
        _in_proj_body,
        grid=(t // tm, n // tn),
        in_specs=[
            pl.BlockSpec((tm, d), lambda i, j: (i, 0)),
            pl.BlockSpec((1, d), lambda i, j: (0, 0)),
            pl.BlockSpec((None, d, tn), lambda i, j: (layer, 0, j)),
        ],
        out_specs=pl.BlockSpec((tm, tn), lambda i, j: (i, j)),
        out_shape=jax.ShapeDtypeStruct((t, n), F32),
        scratch_shapes=[pltpu.VMEM((tm, d), BF16)],
        compiler_params=_params("parallel", "arbitrary"),
        name=name,
    )(x, g, w)


def _rwkv_pre_body(cur_ref, tail_ref, bnd_ref, mu_ref, w0_ref, a0_ref, kk_ref, ka_ref, rk_ref,
                   lora_hi_ref, lora_lo_ref, pat_ref,
                   r_o, w_o, k_o, v_o, kk_o, b_o, bonus_o,
                   *, prompt_tiles, tiles_per_seq, sample_len):
    wa_ = r_o.shape[1]
    cur = cur_ref[...]
    i = pl.program_id(0)
    row = lax.broadcasted_iota(jnp.int32, (cur.shape[0], 1), 0)
    carry = jnp.where(i % tiles_per_seq == 0, 0.0, tail_ref[ROW_TILE - 1:ROW_TILE, :])
    prev = jnp.where(row == 0, carry, pltpu.roll(cur, 1, 0))
    seq_start = jnp.logical_and(i >= prompt_tiles, jnp.bitwise_and(row, sample_len - 1) == 0)
    prev = jnp.where(seq_start, bnd_ref[...], prev)
    sh = cur + (prev - cur) * mu_ref[...]
    r = sh[:, :wa_]
    k = sh[:, wa_:2 * wa_]
    v = sh[:, 2 * wa_:3 * wa_]
    wa = sh[:, 3 * wa_:3 * wa_ + 2 * LORA]
    lane = lax.broadcasted_iota(jnp.int32, wa.shape, 1)
    t = jnp.where(lane < LORA, jnp.tanh(wa), wa)
    t1 = t.astype(BF16)
    t2 = (t - t1.astype(F32)).astype(BF16)
    bh = lora_hi_ref[...]
    bl = lora_lo_ref[...]
    lora = (jnp.dot(t1, bl, preferred_element_type=F32)
            + jnp.dot(t2, bh, preferred_element_type=F32)
            + jnp.dot(t1, bh, preferred_element_type=F32))
    z = -(w0_ref[...] + lora[:, :wa_])
    softplus = jnp.maximum(z, 0.0) + jnp.log1p(jnp.exp(-jnp.abs(z)))
    decay = jnp.exp(-jnp.exp(-softplus - 0.5))
    a = _sigmoid(a0_ref[...] + lora[:, wa_:])
    pat = pat_ref[...]
    kk = k * kk_ref[...]
    kk = kk / jnp.maximum(jnp.sqrt(_head_sum(kk * kk, pat)), KK_NORM_FLOOR)
    k2 = k * (1.0 + (a - 1.0) * ka_ref[...])
    r_o[...] = r
    w_o[...] = decay
    k_o[...] = k2
    v_o[...] = v
    kk_o[...] = kk
    b_o[...] = kk * a
    bonus_o[...] = _head_sum(r * k2 * rk_ref[...], pat) * v


def _rwkv_pre(cur, bnd, mu, w0, a0, k_k, k_a, r_k, lora_hi, lora_lo, layer, pat, *, tm, prompt_rows, seq, sample_len):
    t, sw = cur.shape
    wa_ = w0.shape[1]
    assert prompt_rows % tm == 0 and seq % tm == 0 and tm % sample_len == 0 and tm % ROW_TILE == 0
    assert sample_len & (sample_len - 1) == 0
    prompt_tiles = prompt_rows // tm
    row = lambda i: (i, 0)
    fixed = lambda i: (0, 0)
    vec = pl.BlockSpec((1, wa_), fixed)
    lora_spec = pl.BlockSpec((None,) + lora_hi.shape[1:], lambda i: (layer, 0, 0))
    out = jax.ShapeDtypeStruct((t, wa_), F32)
    return pl.pallas_call(
        functools.partial(_rwkv_pre_body, prompt_tiles=prompt_tiles, tiles_per_seq=seq // tm,
                          sample_len=sample_len),
        grid=(t // tm,),
        in_specs=[
            pl.BlockSpec((tm, sw), row),
            pl.BlockSpec((ROW_TILE, sw), lambda i: (jnp.maximum(i * (tm // ROW_TILE) - 1, 0), 0)),
            pl.BlockSpec((tm, sw), lambda i: (jnp.maximum(i - prompt_tiles, 0), 0)),
            pl.BlockSpec((1, sw), fixed),
            vec, vec, vec, vec, vec,
            lora_spec, lora_spec,
            pl.BlockSpec(pat.shape, fixed),
        ],
        out_specs=[pl.BlockSpec((tm, wa_), row)] * 7,
        out_shape=[out] * 7,
        compiler_params=_params("parallel"),
        name="rwkv_pre",
    )(cur, cur, bnd, mu, w0, a0, k_k, k_a, r_k, lora_hi, lora_lo, pat)


def _rows_to_lanes(x_refs, z_ref):
    for n, x_ref in enumerate(x_refs):
        for c in range(x_ref.shape[1] // V7X_LANES):
            cols = slice(c * V7X_LANES, (c + 1) * V7X_LANES)
            z_ref[n, cols, :] = x_ref[:, cols].T


def _to_lanes_key_body(*refs, n_seq, heads, dup):
    x_refs, o_ref, z_ref = refs[:n_seq], refs[n_seq], refs[n_seq + 1]
    _rows_to_lanes(x_refs, z_ref)
    for k in range(o_ref.shape[0]):
        rows = slice(k * heads, (k + 1) * heads)
        a = jnp.concatenate([z_ref[n, rows, :] for n in range(n_seq)] * dup, axis=0)
        o_ref[k] = a.T


def _to_lanes_val_body(*refs, n_seq, heads, dup):
    x_refs, o_ref, z_ref = refs[:n_seq], refs[n_seq], refs[n_seq + 1]
    _rows_to_lanes(x_refs, z_ref)
    vl = HEAD_A // dup
    for v in range(vl):
        a = jnp.concatenate([z_ref[n, (v * dup + vh) * heads:(v * dup + vh + 1) * heads, :]
                             for vh in range(dup) for n in range(n_seq)], axis=0)
        o_ref[pl.ds(v, T_BLOCK, stride=vl), :] = a.T


def _from_lanes_val_body(y_ref, o_ref, z_ref, *, n_seq, heads, dup):
    vl = HEAD_A // dup
    for v in range(vl):
        a = y_ref[pl.ds(v, T_BLOCK, stride=vl), :].T
        i = 0
        for vh in range(dup):
            for n in range(n_seq):
                z_ref[n, (v * dup + vh) * heads:(v * dup + vh + 1) * heads, :] = a[i * heads:(i + 1) * heads, :]
                i += 1
    for n in range(n_seq):
        for c in range(o_ref.shape[2] // V7X_LANES):
            cols = slice(c * V7X_LANES, (c + 1) * V7X_LANES)
            o_ref[n, :, cols] = z_ref[n, cols, :].T


def _seq_specs(n_seq, length, width):
    blocks = length // T_BLOCK
    return [pl.BlockSpec((T_BLOCK, width), lambda j, n=n: (n * blocks + j, 0)) for n in range(n_seq)]


def _to_lanes_key(x, n_seq, length, heads, dup):
    width = heads * HEAD_A
    assert n_seq * heads * dup == V7X_LANES and length % T_BLOCK == 0
    return pl.pallas_call(
        functools.partial(_to_lanes_key_body, n_seq=n_seq, heads=heads, dup=dup),
        grid=(length // T_BLOCK,),
        in_specs=_seq_specs(n_seq, length, width),
        out_specs=pl.BlockSpec((HEAD_A, T_BLOCK, V7X_LANES), lambda j: (0, j, 0)),
        out_shape=jax.ShapeDtypeStruct((HEAD_A, length, V7X_LANES), F32),
        scratch_shapes=[pltpu.VMEM((n_seq, width, T_BLOCK), F32)],
        compiler_params=_params("parallel"),
        name="to_lanes_key",
    )(*([x] * n_seq))


def _to_lanes_val(x, n_seq, length, heads, dup):
    width = heads * HEAD_A
    vl = HEAD_A // dup
    assert n_seq * heads * dup == V7X_LANES and length % T_BLOCK == 0
    out = pl.pallas_call(
        functools.partial(_to_lanes_val_body, n_seq=n_seq, heads=heads, dup=dup),
        grid=(length // T_BLOCK,),
        in_specs=_seq_specs(n_seq, length, width),
        out_specs=pl.BlockSpec((T_BLOCK * vl, V7X_LANES), lambda j: (j, 0)),
        out_shape=jax.ShapeDtypeStruct((length * vl, V7X_LANES), F32),
        scratch_shapes=[pltpu.VMEM((n_seq, width, T_BLOCK), F32)],
        compiler_params=_params("parallel"),
        name="to_lanes_val",
    )(*([x] * n_seq))
    return out.reshape(length, vl, V7X_LANES)


def _from_lanes_val(y, n_seq, heads, dup):
    length, vl, _ = y.shape
    width = heads * HEAD_A
    out = pl.pallas_call(
        functools.partial(_from_lanes_val_body, n_seq=n_seq, heads=heads, dup=dup),
        grid=(length // T_BLOCK,),
        in_specs=[pl.BlockSpec((T_BLOCK * vl, V7X_LANES), lambda j: (j, 0))],
        out_specs=pl.BlockSpec((n_seq, T_BLOCK, width), lambda j: (0, j, 0)),
        out_shape=jax.ShapeDtypeStruct((n_seq, length, width), F32),
        scratch_shapes=[pltpu.VMEM((n_seq, width, T_BLOCK), F32)],
        compiler_params=_params("parallel"),
        name="from_lanes_val",
    )(y.reshape(length * vl, V7X_LANES))
    return out.reshape(n_seq * length, width)


def _rwkv_rec_body(kk_ref, w_ref, b_ref, k_ref, r_ref, v_ref, s0_ref, y_ref, s_ref, *, key_major):
    n_key = s_ref.shape[0]

    def key_row(ref, k, t):
        return ref[k, pl.ds(t, 1), :] if key_major else ref[t, k:k + 1, :]

    @pl.when(pl.program_id(1) == 0)
    def _():
        s_ref[...] = s0_ref[...]

    def tree(parts):
        while len(parts) > 1:
            parts = [parts[i] + parts[i + 1] for i in range(0, len(parts), 2)]
        return parts[0]

    def step(t, carry):
        acc = [None] * 4
        for k in range(n_key):
            term = s_ref[k] * key_row(kk_ref, k, t)
            acc[k % 4] = term if acc[k % 4] is None else acc[k % 4] + term
        sa = tree(acc)
        vt = v_ref[t]
        acc = [None] * 4
        for k in range(n_key):
            s_new = (s_ref[k] * key_row(w_ref, k, t)
                     + (vt * key_row(k_ref, k, t) - sa * key_row(b_ref, k, t)))
            s_ref[k] = s_new
            term = s_new * key_row(r_ref, k, t)
            acc[k % 4] = term if acc[k % 4] is None else acc[k % 4] + term
        y_ref[t] = tree(acc)
        return carry

    lax.fori_loop(0, v_ref.shape[0], step, 0, unroll=RWKV_STEPS_PER_ITER)


def _rwkv_rec(kk, w, b, k, r, v, s0, tb, key_major):
    length, vl, lanes = v.shape
    n_key = s0.shape[0]
    if key_major:
        key_spec = pl.BlockSpec((n_key, tb, V7X_LANES), lambda g, j: (0, j, g))
    else:
        key_spec = pl.BlockSpec((tb, n_key, V7X_LANES), lambda g, j: (j, 0, g))
    val_spec = pl.BlockSpec((tb, vl, V7X_LANES), lambda g, j: (j, 0, g))
    st_spec = pl.BlockSpec((n_key, vl, V7X_LANES), lambda g, j: (0, 0, g))
    return pl.pallas_call(
        functools.partial(_rwkv_rec_body, key_major=key_major),
        grid=(lanes // V7X_LANES, length // tb),
        in_specs=[key_spec] * 5 + [val_spec, st_spec],
        out_specs=[val_spec, st_spec],
        out_shape=[jax.ShapeDtypeStruct(v.shape, F32), jax.ShapeDtypeStruct(s0.shape, F32)],
        compiler_params=_params("parallel", "arbitrary"),
        name="rwkv_rec",
    )(kk, w, b, k, r, v, s0)


def _cumsum_rows(x):
    n = x.shape[0]
    row = lax.broadcasted_iota(jnp.int32, x.shape, 0)
    sh = 1
    while sh < n:
        x = x + jnp.where(row >= sh, pltpu.roll(x, sh, 0), 0.0)
        sh *= 2
    return x


def _mm(a, b, dims):
    return lax.dot_general(a.astype(BF16), b.astype(BF16), (dims, ((), ())),
                           preferred_element_type=F32)


def _mm_state(a, st, dims):
    a1 = a.astype(BF16)
    s1 = st.astype(BF16)
    s2 = (st - s1.astype(F32)).astype(BF16)
    dn = (dims, ((), ()))
    dot = functools.partial(lax.dot_general, dimension_numbers=dn, preferred_element_type=F32)
    return dot(a1, s2) + dot(a1, s1)


def _gla_chunk(qp, f, iv, lb, st, sub):
    c = qp.shape[0]
    q = qp * _sigmoid(qp)
    kb = jnp.minimum((1.0 - lb) * _sigmoid(-f), MAX_INPUT_GATE)
    b = _cumsum_rows(jnp.log1p(-kb))
    o_inter = _mm_state(q * jnp.exp(b), st, ((1,), (1,)))
    outs = []
    for i in range(c // sub):
        lo = i * sub
        qi = q[lo:lo + sub]
        bi = b[lo:lo + sub]
        oi = o_inter[lo:lo + sub]
        tiles = []
        for r0 in range(0, sub, ROW_TILE):
            rt = min(ROW_TILE, sub - r0)
            q_t, b_t, o_t = qi[r0:r0 + rt], bi[r0:r0 + rt], oi[r0:r0 + rt]
            row = lax.broadcasted_iota(jnp.int32, (rt, 1), 0) + r0
            for s in range(min(sub, r0 + rt)):
                bs = b[lo + s:lo + s + 1]
                e = jnp.exp(jnp.minimum(b_t - bs, 0.0))
                att = jnp.sum(q_t * e * kb[lo + s:lo + s + 1], axis=-1, keepdims=True)
                if s > r0:
                    att = jnp.where(row >= s, att, 0.0)
                o_t = o_t + att * iv[lo + s:lo + s + 1]
            tiles.append(o_t)
        oi = tiles[0] if len(tiles) == 1 else jnp.concatenate(tiles, axis=0)
        if i > 0:
            b_edge = b[lo - 1:lo]
            qt = qi * jnp.exp(bi - b_edge)
            kh = kb[:lo] * jnp.exp(b_edge - b[:lo])
            att = _mm(qt, kh, ((1,), (1,)))
            oi = oi + _mm(att, iv[:lo], ((1,), (0,)))
        outs.append(oi)
    b_last = b[c - 1:c]
    kt = kb * jnp.exp(b_last - b)
    st_new = st * jnp.exp(b_last) + _mm(iv, kt, ((0,), (0,)))
    return jnp.concatenate(outs, axis=0), st_new


def _hgrn_body(q_ref, f_ref, i_ref, lb_ref, s0_ref, o_ref, s_ref, st_ref, *, n_seq, chunk, sub):
    n_heads = q_ref.shape[1] // HEAD_B
    pairs = [(sq, h) for h in range(n_heads) for sq in range(n_seq)]

    @pl.when(pl.program_id(1) == 0)
    def _():
        for sq, h in pairs:
            st_ref[sq, h] = s0_ref[sq, h].T

    for sq, h in pairs:
        col = slice(h * HEAD_B, (h + 1) * HEAD_B)
        rows = slice(sq * chunk, (sq + 1) * chunk)
        o, st = _gla_chunk(q_ref[rows, col], f_ref[rows, col], i_ref[rows, col], lb_ref[:, col],
                           st_ref[sq, h], sub)
        o_ref[rows, col] = o
        st_ref[sq, h] = st

    @pl.when(pl.program_id(1) == pl.num_programs(1) - 1)
    def _():
        for sq, h in pairs:
            s_ref[sq, h] = st_ref[sq, h].T


def _hgrn(pb, lb, s0, layer, *, row_off, n_seq_total, length, chunk, n_seq, col_q, col_f, col_i):
    wb = lb.shape[1]
    n_heads = wb // HEAD_B
    rows = n_seq * chunk
    n_chunks = length // chunk
    assert n_seq == 1 or n_chunks == 1
    assert row_off % rows == 0 and n_seq_total % n_seq == 0
    base = row_off // rows

    def tok(cb):
        return pl.BlockSpec((rows, wb), lambda g, c: (base + g * n_chunks + c, cb))

    st_block = (n_seq, n_heads, HEAD_B, HEAD_B)
    return pl.pallas_call(
        functools.partial(_hgrn_body, n_seq=n_seq, chunk=chunk, sub=min(GLA_SUB, chunk)),
        grid=(n_seq_total // n_seq, n_chunks),
        in_specs=[tok(col_q), tok(col_f), tok(col_i), pl.BlockSpec((1, wb), lambda g, c: (0, 0)),
                  pl.BlockSpec((None,) + st_block, lambda g, c: (layer, g, 0, 0, 0))],
        out_specs=[pl.BlockSpec((rows, wb), lambda g, c: (g * n_chunks + c, 0)),
                   pl.BlockSpec(st_block, lambda g, c: (g, 0, 0, 0))],
        out_shape=[jax.ShapeDtypeStruct((n_seq_total * length, wb), F32),
                   jax.ShapeDtypeStruct(s0.shape[1:], F32)],
        scratch_shapes=[pltpu.VMEM(st_block, F32)],
        compiler_params=_params("parallel", "arbitrary"),
        name="hgrn",
    )(pb, pb, pb, lb, s0)


def _out_proj_body(x_ref, yrec_ref, bonus_ref, za_ref, ob_ref, zb_ref, ga_ref, gb_ref,
                   lnw_ref, lnb_ref, hg_ref, pat_ref, pa_ref, pb_ref, wo_ref, y_ref):
    pat = pat_ref[...]
    inv_head = 1.0 / HEAD_A
    y = yrec_ref[...]
    mu = _head_sum(y, pat) * inv_head
    yc = y - mu
    var = _head_sum(yc * yc, pat) * inv_head
    za = za_ref[...]
    ya = (yc * lax.rsqrt(var + GN_EPS) * lnw_ref[...] + lnb_ref[...] + bonus_ref[...]) * (za * _sigmoid(za))
    ob = ob_ref[...]
    zb = zb_ref[...]
    hg = hg_ref[...]
    parts = []
    for g in range(ob.shape[1] // HEAD_B):
        og = ob[:, g * HEAD_B:(g + 1) * HEAD_B]
        ms = jnp.mean(og * og, axis=-1, keepdims=True)
        parts.append(og * lax.rsqrt(ms + NORM_EPS) * hg)
    yb = jnp.concatenate(parts, axis=1) * (zb * _sigmoid(zb))
    merged = (_sigmoid(ga_ref[...]) * jnp.dot(ya.astype(BF16), pa_ref[...], preferred_element_type=F32)
              + _sigmoid(gb_ref[...]) * jnp.dot(yb.astype(BF16), pb_ref[...], preferred_element_type=F32))
    y_ref[...] = x_ref[...] + jnp.dot(merged.astype(BF16), wo_ref[...], preferred_element_type=F32)


def _out_proj(x, yrec, bonus, pb, ob, ln_w, ln_b, hg_g, pat, proj_a, proj_b, w_out, layer,
              *, tm, col_za, col_zb, col_ga, col_gb):
    t, d = x.shape
    wa_ = yrec.shape[1]
    row = lambda i: (i, 0)
    fixed = lambda i: (0, 0)
    half = lambda cb: pl.BlockSpec((tm, wa_), lambda i: (i, cb))
    full = lambda cb: pl.BlockSpec((tm, d), lambda i: (i, cb))
    const = lambda a: pl.BlockSpec(a.shape, fixed, pipeline_mode=pl.Buffered(1))
    weight = lambda a: pl.BlockSpec((None,) + a.shape[1:], lambda i: (layer, 0, 0),
                                    pipeline_mode=pl.Buffered(1))
    return pl.pallas_call(
        _out_proj_body,
        grid=(t // tm,),
        in_specs=[
            pl.BlockSpec((tm, d), row), half(0), half(0), half(col_za), half(0), half(col_zb),
            full(col_ga), full(col_gb),
            const(ln_w), const(ln_b), const(hg_g), const(pat),
            weight(proj_a), weight(proj_b), weight(w_out),
        ],
        out_specs=pl.BlockSpec((tm, d), row),
        out_shape=jax.ShapeDtypeStruct((t, d), F32),
        compiler_params=_params("parallel"),
        name="out_proj",
    )(x, yrec, bonus, pb, ob, pb, pb, pb, ln_w, ln_b, hg_g, pat, proj_a, proj_b, w_out)


def _final_norm_body(x_ref, g_ref, o_ref):
    x = x_ref[...]
    ms = jnp.mean(x * x, axis=-1, keepdims=True)
    o_ref[...] = x * lax.rsqrt(ms + NORM_EPS) * g_ref[...]


def _final_norm(x, g, tm):
    t, d = x.shape
    return pl.pallas_call(
        _final_norm_body,
        grid=(t // tm,),
        in_specs=[pl.BlockSpec((tm, d), lambda i: (i, 0)), pl.BlockSpec((1, d), lambda i: (0, 0))],
        out_specs=pl.BlockSpec((tm, d), lambda i: (i, 0)),
        out_shape=jax.ShapeDtypeStruct((t, d), F32),
        compiler_params=_params("parallel"),
        name="final_norm",
    )(x, g)


def _perm_key(a, heads):
    lead = a.shape[:-1]
    return jnp.swapaxes(a.reshape(*lead, heads, HEAD_A), -1, -2).reshape(*lead, heads * HEAD_A)


def _unperm_key(a, heads):
    lead = a.shape[:-1]
    return jnp.swapaxes(a.reshape(*lead, HEAD_A, heads), -1, -2).reshape(*lead, heads * HEAD_A)


def _perm_val(a, heads, dup):
    lead = a.shape[:-1]
    x = a.reshape(*lead, heads, dup, HEAD_A // dup)
    return jnp.moveaxis(x, (-3, -2, -1), (-1, -2, -3)).reshape(*lead, heads * HEAD_A)


def _unperm_val(a, heads, dup):
    lead = a.shape[:-1]
    x = a.reshape(*lead, HEAD_A // dup, dup, heads)
    return jnp.moveaxis(x, (-3, -2, -1), (-1, -2, -3)).reshape(*lead, heads * HEAD_A)


def _perm_shift(a, heads, dup):
    w = heads * HEAD_A
    return jnp.concatenate([_perm_key(a[..., :w], heads), _perm_key(a[..., w:2 * w], heads),
                            _perm_val(a[..., 2 * w:3 * w], heads, dup), a[..., 3 * w:]], axis=-1)


def _unperm_shift(a, heads, dup):
    w = heads * HEAD_A
    return jnp.concatenate([_unperm_key(a[..., :w], heads), _unperm_key(a[..., w:2 * w], heads),
                            _unperm_val(a[..., 2 * w:3 * w], heads, dup), a[..., 3 * w:]], axis=-1)


def kernel(x_prompt, x_sample, state_rwkv, state_hgrn, state_shift, norm_g, w_in, shift_mu, rwkv_w0, rwkv_w2, rwkv_a0, rwkv_a2, rwkv_k_k, rwkv_k_a, rwkv_r_k, rwkv_ln_w, rwkv_ln_b, hgrn_lb_logits, hgrn_norm_g, proj_a, proj_b, w_out, final_norm_g):
    nb, seq, d = x_prompt.shape
    db, dseq, _ = x_sample.shape
    depth = w_in.shape[0]
    wa_ = rwkv_w0.shape[1]
    wb_ = hgrn_lb_logits.shape[1]
    heads_a = wa_ // HEAD_A
    heads_b = wb_ // HEAD_B
    sw = 3 * wa_ + 2 * LORA
    assert w_in.shape[2] == sw + wa_ + 4 * wb_ + 2 * d and wa_ == wb_ and d == 2 * wa_
    dup = V7X_LANES // (nb * heads_a)
    assert nb * heads_a * dup == V7X_LANES and (db * heads_a) % V7X_LANES == 0
    vl = HEAD_A // dup
    tp, ts = nb * seq, db * dseq
    t_all = tp + ts
    pk = functools.partial(_perm_key, heads=heads_a)
    pv = functools.partial(_perm_val, heads=heads_a, dup=dup)

    off = sw
    za_w = pv(w_in[:, :, off:off + wa_])
    qb_w = w_in[:, :, off + wa_:off + wa_ + wb_]
    fb_w = w_in[:, :, off + wa_ + wb_:off + wa_ + 2 * wb_]
    ib_w = w_in[:, :, off + wa_ + 2 * wb_:off + wa_ + 3 * wb_]
    zb_w = w_in[:, :, off + wa_ + 3 * wb_:off + wa_ + 4 * wb_]
    ga_w = w_in[:, :, off + wa_ + 4 * wb_:off + wa_ + 4 * wb_ + d]
    gb_w = w_in[:, :, off + wa_ + 4 * wb_ + d:]
    w_first = _perm_shift(w_in[:, :, :sw], heads_a, dup).astype(BF16)
    w_second = jnp.concatenate([ga_w, gb_w, za_w, qb_w, fb_w, ib_w, zb_w], axis=2).astype(BF16)
    col_ga, col_gb = 0, 1
    col_za, col_q, col_f, col_i, col_zb = 4, 5, 6, 7, 8
    proj_a_h = (proj_a.reshape(depth, heads_a, dup, vl, d).transpose(0, 3, 2, 1, 4)
                .reshape(depth, wa_, d).astype(BF16))
    proj_b_h, w_out_h = proj_b.astype(BF16), w_out.astype(BF16)
    mu_p = _perm_shift(shift_mu, heads_a, dup)
    w0_p, a0_p, kk_p, ka_p = pk(rwkv_w0), pk(rwkv_a0), pk(rwkv_k_k), pk(rwkv_k_a)
    rk_p = pk(rwkv_r_k.reshape(depth, wa_))
    lnw_p, lnb_p = pv(rwkv_ln_w), pv(rwkv_ln_b)

    zero = jnp.zeros((depth, LORA, wa_), F32)
    lora = jnp.concatenate([jnp.concatenate([pk(rwkv_w2), zero], axis=2),
                            jnp.concatenate([zero, pk(rwkv_a2)], axis=2)], axis=1)
    lora_hi = lora.astype(BF16)
    lora_lo = (lora - lora_hi.astype(F32)).astype(BF16)
    lane_head = jnp.arange(V7X_LANES) % heads_a
    pat = (lane_head[:, None] == lane_head[None, :]).astype(BF16)

    lbs = _lower_bounds(hgrn_lb_logits)

    tm_in = _tile(t_all, 1024)
    tm_pre = _tile(math.gcd(seq, ts), 256)
    tm_out = _tile(t_all, 256)
    tb_p = _tile(seq, 32)
    chunk_p = _tile(seq, 64)
    ns_s = _tile(db, 2, 1)

    x = jnp.concatenate([x_prompt.reshape(tp, d), x_sample.reshape(ts, d)], axis=0)
    zero_a = jnp.zeros((HEAD_A, vl, V7X_LANES), F32)
    zero_b = jnp.zeros((1, nb, heads_b, HEAD_B, HEAD_B), F32)
    shift_in = jnp.pad(_perm_shift(state_shift, heads_a, dup)[:, :, None, :],
                       ((0, 0), (0, 0), (0, dseq - 1), (0, 0))).reshape(depth, ts, sw)
    lanes_s = db * heads_a
    outs = [[] for _ in range(6)]
    for l in range(depth):
        row = lambda a: a[l][None]
        p1 = _in_proj(x, row(norm_g), w_first, l, tm_in, _tile(sw, 640, V7X_LANES), "in_proj_shift")
        p2 = _in_proj(x, row(norm_g), w_second, l, tm_in, _tile(w_second.shape[2], 512, V7X_LANES), "in_proj_rest")

        r, w, k, v, kk, b, bonus = _rwkv_pre(
            p1, shift_in[l], row(mu_p), row(w0_p), row(a0_p), row(kk_p), row(ka_p), row(rk_p),
            lora_hi, lora_lo, l, pat, tm=tm_pre, prompt_rows=tp, seq=seq, sample_len=dseq)

        keyed = [_to_lanes_key(a, nb, seq, heads_a, dup) for a in (kk, w, b, k, r)]
        y_p, sa_p = _rwkv_rec(*keyed, _to_lanes_val(v, nb, seq, heads_a, dup), zero_a, tb_p, True)
        y_p = _from_lanes_val(y_p, nb, heads_a, dup)
        sa_p = (sa_p.reshape(HEAD_A, vl, dup, nb, heads_a).transpose(3, 4, 2, 1, 0)
                .reshape(nb, heads_a, HEAD_A, HEAD_A))
        to_s = lambda a: a[tp:].reshape(db, dseq * wa_).T.reshape(dseq, HEAD_A, lanes_s)
        s0_s = (state_rwkv[l].reshape(db, heads_a, dup, vl, HEAD_A).transpose(4, 3, 2, 1, 0)
                .reshape(HEAD_A, HEAD_A, lanes_s))
        y_s, sa_s = _rwkv_rec(*[to_s(a) for a in (kk, w, b, k, r, v)], s0_s, dseq, False)
        y_s = y_s.reshape(dseq * wa_, db).T.reshape(ts, wa_)
        sa_s = (sa_s.reshape(HEAD_A, vl, dup, heads_a, db).transpose(4, 3, 2, 1, 0)
                .reshape(db, heads_a, HEAD_A, HEAD_A))
        yrec = jnp.concatenate([y_p, y_s], axis=0)

        cols = dict(col_q=col_q, col_f=col_f, col_i=col_i)
        o_p, sb_p = _hgrn(p2, row(lbs), zero_b, 0, row_off=0, n_seq_total=nb, length=seq,
                          chunk=chunk_p, n_seq=1, **cols)
        o_s, sb_s = _hgrn(p2, row(lbs), state_hgrn, l, row_off=tp, n_seq_total=db,
                          length=dseq, chunk=dseq, n_seq=ns_s, **cols)
        ob = jnp.concatenate([o_p, o_s], axis=0)

        x = _out_proj(x, yrec, bonus, p2, ob, row(lnw_p), row(lnb_p), row(hgrn_norm_g), pat,
                      proj_a_h, proj_b_h, w_out_h, l,
                      tm=tm_out, col_za=col_za, col_zb=col_zb, col_ga=col_ga, col_gb=col_gb)

        last_p = _unperm_shift(p1[seq - 1:tp:seq], heads_a, dup)
        last_s = _unperm_shift(p1[tp + dseq - 1::dseq], heads_a, dup)
        for dst, val in zip(outs, (sa_p, sb_p, last_p, sa_s, sb_s, last_s)):
            dst.append(val)

    y = _final_norm(x, final_norm_g[None], _tile(t_all, 512))
    return (y[:tp].reshape(nb, seq, d), y[tp:].reshape(db, dseq, d)) + tuple(jnp.stack(o) for o in outs)
```

```python
import functools
import math

import jax
import jax.numpy as jnp
from jax import lax
from jax.experimental import pallas as pl
from jax.experimental.pallas import tpu as pltpu

F32 = jnp.float32
BF16 = jnp.bfloat16

NORM_EPS = 1e-6
GN_EPS = 64e-5
MAX_INPUT_GATE = 1.0 - 1e-6
KK_NORM_FLOOR = 1e-12

HEAD_A = 64
HEAD_B = 128
LORA = 64
V7X_LANES = 128
V7X_VMEM_BYTES = 64 * 1024 * 1024
VMEM_LIMIT = (V7X_VMEM_BYTES * 7) // 8
GLA_SUB = 16
ROW_TILE = 8
RWKV_STEPS_PER_ITER = 8
T_BLOCK = V7X_LANES


def _tile(total, pref, mult=8):
    best = None
    for t in range(mult, min(total, pref) + 1, mult):
        if total % t == 0:
            best = t
    assert best is not None, (total, pref, mult)
    return best


def _params(*sem):
    return pltpu.CompilerParams(dimension_semantics=sem, vmem_limit_bytes=VMEM_LIMIT)


def _sigmoid(x):
    return 1.0 / (1.0 + jnp.exp(-x))


def _split3(x):
    h1 = x.astype(BF16)
    r1 = x - h1.astype(F32)
    h2 = r1.astype(BF16)
    h3 = (r1 - h2.astype(F32)).astype(BF16)
    return h1, h2, h3


def _head_sum(x, pat):
    groups = x.shape[1] // V7X_LANES
    acc = None
    for g in range(groups):
        for part in reversed(_split3(x[:, g * V7X_LANES:(g + 1) * V7X_LANES])):
            d = jnp.dot(part, pat, preferred_element_type=F32)
            acc = d if acc is None else acc + d
    return jnp.concatenate([acc] * groups, axis=1)


def _lower_bound_body(logits_ref, lb_ref):
    x = logits_ref[...]
    e = jnp.exp(x - jnp.max(x, axis=0, keepdims=True))
    probs = e / jnp.sum(e, axis=0, keepdims=True)
    acc = jnp.zeros_like(probs[0:1])
    lb_ref[0:1, :] = acc
    for l in range(1, x.shape[0]):
        acc = acc + probs[l:l + 1]
        lb_ref[l:l + 1, :] = acc


def _lower_bounds(logits):
    return pl.pallas_call(
        _lower_bound_body,
        out_shape=jax.ShapeDtypeStruct(logits.shape, F32),
        name="hgrn_lower_bounds",
    )(logits)


def _in_proj_body(x_ref, g_ref, w_ref, o_ref, h_ref):
    @pl.when(pl.program_id(1) == 0)
    def _():
        x = x_ref[...]
        ms = jnp.mean(x * x, axis=-1, keepdims=True)
        h_ref[...] = (x * lax.rsqrt(ms + NORM_EPS) * g_ref[...]).astype(BF16)

    o_ref[...] = jnp.dot(h_ref[...], w_ref[...], preferred_element_type=F32)


def _in_proj(x, g, w, layer, tm, tn, name):
    t, d = x.shape
    n = w.shape[2]
    return pl.pallas_call(
        _in_proj_body,
        grid=(t // tm, n // tn),
        in_specs=[
            pl.BlockSpec((tm, d), lambda i, j: (i, 0)),
            pl.BlockSpec((1, d), lambda i, j: (0, 0)),
            pl.BlockSpec((None, d, tn), lambda i, j: (layer, 0, j)),
        ],
        out_specs=pl.BlockSpec((tm, tn), lambda i, j: (i, j)),
        out_shape=jax.ShapeDtypeStruct((t, n), F32),
        scratch_shapes=[pltpu.VMEM((tm, d), BF16)],
        compiler_params=_params("parallel", "arbitrary"),
        name=name,
    )(x, g, w)


def _rwkv_pre_body(cur_ref, tail_ref, bnd_ref, mu_ref, w0_ref, a0_ref, kk_ref, ka_ref, rk_ref,
                   lora_hi_ref, lora_lo_ref, pat_ref,
                   r_o, w_o, k_o, v_o, kk_o, b_o, bonus_o,
                   *, prompt_tiles, tiles_per_seq, sample_len):
    wa_ = r_o.shape[1]
    cur = cur_ref[...]
    i = pl.program_id(0)
    row = lax.broadcasted_iota(jnp.int32, (cur.shape[0], 1), 0)
    carry = jnp.where(i % tiles_per_seq == 0, 0.0, tail_ref[ROW_TILE - 1:ROW_TILE, :])
    prev = jnp.where(row == 0, carry, pltpu.roll(cur, 1, 0))
    seq_start = jnp.logical_and(i >= prompt_tiles, jnp.bitwise_and(row, sample_len - 1) == 0)
    prev = jnp.where(seq_start, bnd_ref[...], prev)
    sh = cur + (prev - cur) * mu_ref[...]
    r = sh[:, :wa_]
    k = sh[:, wa_:2 * wa_]
    v = sh[:, 2 * wa_:3 * wa_]
    wa = sh[:, 3 * wa_:3 * wa_ + 2 * LORA]
    lane = lax.broadcasted_iota(jnp.int32, wa.shape, 1)
    t = jnp.where(lane < LORA, jnp.tanh(wa), wa)
    t1 = t.astype(BF16)
    t2 = (t - t1.astype(F32)).astype(BF16)
    bh = lora_hi_ref[...]
    bl = lora_lo_ref[...]
    lora = (jnp.dot(t1, bl, preferred_element_type=F32)
            + jnp.dot(t2, bh, preferred_element_type=F32)
            + jnp.dot(t1, bh, preferred_element_type=F32))
    z = -(w0_ref[...] + lora[:, :wa_])
    softplus = jnp.maximum(z, 0.0) + jnp.log1p(jnp.exp(-jnp.abs(z)))
    decay = jnp.exp(-jnp.exp(-softplus - 0.5))
    a = _sigmoid(a0_ref[...] + lora[:, wa_:])
    pat = pat_ref[...]
    kk = k * kk_ref[...]
    kk = kk / jnp.maximum(jnp.sqrt(_head_sum(kk * kk, pat)), KK_NORM_FLOOR)
    k2 = k * (1.0 + (a - 1.0) * ka_ref[...])
    r_o[...] = r
    w_o[...] = decay
    k_o[...] = k2
    v_o[...] = v
    kk_o[...] = kk
    b_o[...] = kk * a
    bonus_o[...] = _head_sum(r * k2 * rk_ref[...], pat) * v


def _rwkv_pre(cur, bnd, mu, w0, a0, k_k, k_a, r_k, lora_hi, lora_lo, layer, pat, *, tm, prompt_rows, seq, sample_len):
    t, sw = cur.shape
    wa_ = w0.shape[1]
    assert prompt_rows % tm == 0 and seq % tm == 0 and tm % sample_len == 0 and tm % ROW_TILE == 0
    assert sample_len & (sample_len - 1) == 0
    prompt_tiles = prompt_rows // tm
    row = lambda i: (i, 0)
    fixed = lambda i: (0, 0)
    vec = pl.BlockSpec((1, wa_), fixed)
    lora_spec = pl.BlockSpec((None,) + lora_hi.shape[1:], lambda i: (layer, 0, 0))
    out = jax.ShapeDtypeStruct((t, wa_), F32)
    return pl.pallas_call(
        functools.partial(_rwkv_pre_body, prompt_tiles=prompt_tiles, tiles_per_seq=seq // tm,
                          sample_len=sample_len),
        grid=(t // tm,),
        in_specs=[
            pl.BlockSpec((tm, sw), row),
            pl.BlockSpec((ROW_TILE, sw), lambda i: (jnp.maximum(i * (tm // ROW_TILE) - 1, 0), 0)),
            pl.BlockSpec((tm, sw), lambda i: (jnp.maximum(i - prompt_tiles, 0), 0)),
            pl.BlockSpec((1, sw), fixed),
            vec, vec, vec, vec, vec,
            lora_spec, lora_spec,
            pl.BlockSpec(pat.shape, fixed),
        ],
        out_specs=[pl.BlockSpec((tm, wa_), row)] * 7,
        out_shape=[out] * 7,
        compiler_params=_params("parallel"),
        name="rwkv_pre",
    )(cur, cur, bnd, mu, w0, a0, k_k, k_a, r_k, lora_hi, lora_lo, pat)


def _rows_to_lanes(x_refs, z_ref):
    for n, x_ref in enumerate(x_refs):
        for c in range(x_ref.shape[1] // V7X_LANES):
            cols = slice(c * V7X_LANES, (c + 1) * V7X_LANES)
            z_ref[n, cols, :] = x_ref[:, cols].T


def _to_lanes_key_body(*refs, n_seq, heads, dup):
    x_refs, o_ref, z_ref = refs[:n_seq], refs[n_seq], refs[n_seq + 1]
    _rows_to_lanes(x_refs, z_ref)
    for k in range(o_ref.shape[0]):
        rows = slice(k * heads, (k + 1) * heads)
        a = jnp.concatenate([z_ref[n, rows, :] for n in range(n_seq)] * dup, axis=0)
        o_ref[k] = a.T


def _to_lanes_val_body(*refs, n_seq, heads, dup):
    x_refs, o_ref, z_ref = refs[:n_seq], refs[n_seq], refs[n_seq + 1]
    _rows_to_lanes(x_refs, z_ref)
    vl = HEAD_A // dup
    for v in range(vl):
        a = jnp.concatenate([z_ref[n, (v * dup + vh) * heads:(v * dup + vh + 1) * heads, :]
                             for vh in range(dup) for n in range(n_seq)], axis=0)
        o_ref[pl.ds(v, T_BLOCK, stride=vl), :] = a.T


def _from_lanes_val_body(y_ref, o_ref, z_ref, *, n_seq, heads, dup):
    vl = HEAD_A // dup
    for v in range(vl):
        a = y_ref[pl.ds(v, T_BLOCK, stride=vl), :].T
        i = 0
        for vh in range(dup):
            for n in range(n_seq):
                z_ref[n, (v * dup + vh) * heads:(v * dup + vh + 1) * heads, :] = a[i * heads:(i + 1) * heads, :]
                i += 1
    for n in range(n_seq):
        for c in range(o_ref.shape[2] // V7X_LANES):
            cols = slice(c * V7X_LANES, (c + 1) * V7X_LANES)
            o_ref[n, :, cols] = z_ref[n, cols, :].T


def _seq_specs(n_seq, length, width):
    blocks = length // T_BLOCK
    return [pl.BlockSpec((T_BLOCK, width), lambda j, n=n: (n * blocks + j, 0)) for n in range(n_seq)]


def _to_lanes_key(x, n_seq, length, heads, dup):
    width = heads * HEAD_A
    assert n_seq * heads * dup == V7X_LANES and length % T_BLOCK == 0
    return pl.pallas_call(
        functools.partial(_to_lanes_key_body, n_seq=n_seq, heads=heads, dup=dup),
        grid=(length // T_BLOCK,),
        in_specs=_seq_specs(n_seq, length, width),
        out_specs=pl.BlockSpec((HEAD_A, T_BLOCK, V7X_LANES), lambda j: (0, j, 0)),
        out_shape=jax.ShapeDtypeStruct((HEAD_A, length, V7X_LANES), F32),
        scratch_shapes=[pltpu.VMEM((n_seq, width, T_BLOCK), F32)],
        compiler_params=_params("parallel"),
        name="to_lanes_key",
    )(*([x] * n_seq))


def _to_lanes_val(x, n_seq, length, heads, dup):
    width = heads * HEAD_A
    vl = HEAD_A // dup
    assert n_seq * heads * dup == V7X_LANES and length % T_BLOCK == 0
    out = pl.pallas_call(
        functools.partial(_to_lanes_val_body, n_seq=n_seq, heads=heads, dup=dup),
        grid=(length // T_BLOCK,),
        in_specs=_seq_specs(n_seq, length, width),
        out_specs=pl.BlockSpec((T_BLOCK * vl, V7X_LANES), lambda j: (j, 0)),
        out_shape=jax.ShapeDtypeStruct((length * vl, V7X_LANES), F32),
        scratch_shapes=[pltpu.VMEM((n_seq, width, T_BLOCK), F32)],
        compiler_params=_params("parallel"),
        name="to_lanes_val",
    )(*([x] * n_seq))
    return out.reshape(length, vl, V7X_LANES)


def _from_lanes_val(y, n_seq, heads, dup):
    length, vl, _ = y.shape
    width = heads * HEAD_A
    out = pl.pallas_call(
        functools.partial(_from_lanes_val_body, n_seq=n_seq, heads=heads, dup=dup),
        grid=(length // T_BLOCK,),
        in_specs=[pl.BlockSpec((T_BLOCK * vl, V7X_LANES), lambda j: (j, 0))],
        out_specs=pl.BlockSpec((n_seq, T_BLOCK, width), lambda j: (0, j, 0)),
        out_shape=jax.ShapeDtypeStruct((n_seq, length, width), F32),
        scratch_shapes=[pltpu.VMEM((n_seq, width, T_BLOCK), F32)],
        compiler_params=_params("parallel"),
        name="from_lanes_val",
    )(y.reshape(length * vl, V7X_LANES))
    return out.reshape(n_seq * length, width)


def _rwkv_rec_body(kk_ref, w_ref, b_ref, k_ref, r_ref, v_ref, s0_ref, y_ref, s_ref, *, key_major):
    n_key = s_ref.shape[0]

    def key_row(ref, k, t):
        return ref[k, pl.ds(t, 1), :] if key_major else ref[t, k:k + 1, :]

    @pl.when(pl.program_id(1) == 0)
    def _():
        s_ref[...] = s0_ref[...]

    def tree(parts):
        while len(parts) > 1:
            parts = [parts[i] + parts[i + 1] for i in range(0, len(parts), 2)]
        return parts[0]

    def step(t):
        acc = [None] * 4
        for k in range(n_key):
            term = s_ref[k] * key_row(kk_ref, k, t)
            acc[k % 4] = term if acc[k % 4] is None else acc[k % 4] + term
        sa = tree(acc)
        vt = v_ref[t]
        acc = [None] * 4
        for k in range(n_key):
            s_new = (s_ref[k] * key_row(w_ref, k, t)
                     + (vt * key_row(k_ref, k, t) - sa * key_row(b_ref, k, t)))
            s_ref[k] = s_new
            term = s_new * key_row(r_ref, k, t)
            acc[k % 4] = term if acc[k % 4] is None else acc[k % 4] + term
        y_ref[t] = tree(acc)

    def body(it, carry):
        for u in range(RWKV_STEPS_PER_ITER):
            step(it * RWKV_STEPS_PER_ITER + u)
        return carry

    lax.fori_loop(0, v_ref.shape[0] // RWKV_STEPS_PER_ITER, body, 0)


def _rwkv_rec(kk, w, b, k, r, v, s0, tb, key_major):
    length, vl, lanes = v.shape
    n_key = s0.shape[0]
    assert tb % RWKV_STEPS_PER_ITER == 0
    if key_major:
        key_spec = pl.BlockSpec((n_key, tb, V7X_LANES), lambda g, j: (0, j, g))
    else:
        key_spec = pl.BlockSpec((tb, n_key, V7X_LANES), lambda g, j: (j, 0, g))
    val_spec = pl.BlockSpec((tb, vl, V7X_LANES), lambda g, j: (j, 0, g))
    st_spec = pl.BlockSpec((n_key, vl, V7X_LANES), lambda g, j: (0, 0, g))
    return pl.pallas_call(
        functools.partial(_rwkv_rec_body, key_major=key_major),
        grid=(lanes // V7X_LANES, length // tb),
        in_specs=[key_spec] * 5 + [val_spec, st_spec],
        out_specs=[val_spec, st_spec],
        out_shape=[jax.ShapeDtypeStruct(v.shape, F32), jax.ShapeDtypeStruct(s0.shape, F32)],
        compiler_params=_params("parallel", "arbitrary"),
        name="rwkv_rec",
    )(kk, w, b, k, r, v, s0)


def _cumsum_rows(x):
    n = x.shape[0]
    row = lax.broadcasted_iota(jnp.int32, x.shape, 0)
    sh = 1
    while sh < n:
        x = x + jnp.where(row >= sh, pltpu.roll(x, sh, 0), 0.0)
        sh *= 2
    return x


def _mm(a, b, dims):
    return lax.dot_general(a.astype(BF16), b.astype(BF16), (dims, ((), ())),
                           preferred_element_type=F32)


def _mm_state(a, st, dims):
    a1 = a.astype(BF16)
    s1 = st.astype(BF16)
    s2 = (st - s1.astype(F32)).astype(BF16)
    dn = (dims, ((), ()))
    dot = functools.partial(lax.dot_general, dimension_numbers=dn, preferred_element_type=F32)
    return dot(a1, s2) + dot(a1, s1)


def _gla_chunk(qp, f, iv, lb, st, sub):
    c = qp.shape[0]
    q = qp * _sigmoid(qp)
    kb = jnp.minimum((1.0 - lb) * _sigmoid(-f), MAX_INPUT_GATE)
    b = _cumsum_rows(jnp.log1p(-kb))
    o_inter = _mm_state(q * jnp.exp(b), st, ((1,), (1,)))
    outs = []
    for i in range(c // sub):
        lo = i * sub
        qi = q[lo:lo + sub]
        bi = b[lo:lo + sub]
        oi = o_inter[lo:lo + sub]
        tiles = []
        for r0 in range(0, sub, ROW_TILE):
            rt = min(ROW_TILE, sub - r0)
            q_t, b_t, o_t = qi[r0:r0 + rt], bi[r0:r0 + rt], oi[r0:r0 + rt]
            row = lax.broadcasted_iota(jnp.int32, (rt, 1), 0) + r0
            for s in range(min(sub, r0 + rt)):
                bs = b[lo + s:lo + s + 1]
                e = jnp.exp(jnp.minimum(b_t - bs, 0.0))
                att = jnp.sum(q_t * e * kb[lo + s:lo + s + 1], axis=-1, keepdims=True)
                if s > r0:
                    att = jnp.where(row >= s, att, 0.0)
                o_t = o_t + att * iv[lo + s:lo + s + 1]
            tiles.append(o_t)
        oi = tiles[0] if len(tiles) == 1 else jnp.concatenate(tiles, axis=0)
        if i > 0:
            b_edge = b[lo - 1:lo]
            qt = qi * jnp.exp(bi - b_edge)
            kh = kb[:lo] * jnp.exp(b_edge - b[:lo])
            att = _mm(qt, kh, ((1,), (1,)))
            oi = oi + _mm(att, iv[:lo], ((1,), (0,)))
        outs.append(oi)
    b_last = b[c - 1:c]
    kt = kb * jnp.exp(b_last - b)
    st_new = st * jnp.exp(b_last) + _mm(iv, kt, ((0,), (0,)))
    return jnp.concatenate(outs, axis=0), st_new


def _hgrn_body(q_ref, f_ref, i_ref, lb_ref, s0_ref, o_ref, s_ref, st_ref, *, n_seq, chunk, sub):
    n_heads = q_ref.shape[1] // HEAD_B
    pairs = [(sq, h) for h in range(n_heads) for sq in range(n_seq)]

    @pl.when(pl.program_id(1) == 0)
    def _():
        for sq, h in pairs:
            st_ref[sq, h] = s0_ref[sq, h].T

    for sq, h in pairs:
        col = slice(h * HEAD_B, (h + 1) * HEAD_B)
        rows = slice(sq * chunk, (sq + 1) * chunk)
        o, st = _gla_chunk(q_ref[rows, col], f_ref[rows, col], i_ref[rows, col], lb_ref[:, col],
                           st_ref[sq, h], sub)
        o_ref[rows, col] = o
        st_ref[sq, h] = st

    @pl.when(pl.program_id(1) == pl.num_programs(1) - 1)
    def _():
        for sq, h in pairs:
            s_ref[sq, h] = st_ref[sq, h].T


def _hgrn(pb, lb, s0, layer, *, row_off, n_seq_total, length, chunk, n_seq, col_q, col_f, col_i):
    wb = lb.shape[1]
    n_heads = wb // HEAD_B
    rows = n_seq * chunk
    n_chunks = length // chunk
    assert n_seq == 1 or n_chunks == 1
    assert row_off % rows == 0 and n_seq_total % n_seq == 0
    base = row_off // rows

    def tok(cb):
        return pl.BlockSpec((rows, wb), lambda g, c: (base + g * n_chunks + c, cb))

    st_block = (n_seq, n_heads, HEAD_B, HEAD_B)
    return pl.pallas_call(
        functools.partial(_hgrn_body, n_seq=n_seq, chunk=chunk, sub=min(GLA_SUB, chunk)),
        grid=(n_seq_total // n_seq, n_chunks),
        in_specs=[tok(col_q), tok(col_f), tok(col_i), pl.BlockSpec((1, wb), lambda g, c: (0, 0)),
                  pl.BlockSpec((None,) + st_block, lambda g, c: (layer, g, 0, 0, 0))],
        out_specs=[pl.BlockSpec((rows, wb), lambda g, c: (g * n_chunks + c, 0)),
                   pl.BlockSpec(st_block, lambda g, c: (g, 0, 0, 0))],
        out_shape=[jax.ShapeDtypeStruct((n_seq_total * length, wb), F32),
                   jax.ShapeDtypeStruct(s0.shape[1:], F32)],
        scratch_shapes=[pltpu.VMEM(st_block, F32)],
        compiler_params=_params("parallel", "arbitrary"),
        name="hgrn",
    )(pb, pb, pb, lb, s0)


def _out_proj_body(x_ref, yrec_p_ref, yrec_s_ref, bonus_ref, za_ref, ob_p_ref, ob_s_ref, zb_ref, ga_ref, gb_ref,
                   lnw_ref, lnb_ref, hg_ref, pat_ref, pa_ref, pb_ref, wo_ref, y_ref, *, prompt_tiles):
    pat = pat_ref[...]
    inv_head = 1.0 / HEAD_A
    is_prompt = pl.program_id(0) < prompt_tiles
    y = jnp.where(is_prompt, yrec_p_ref[...], yrec_s_ref[...])
    mu = _head_sum(y, pat) * inv_head
    yc = y - mu
    var = _head_sum(yc * yc, pat) * inv_head
    za = za_ref[...]
    ya = (yc * lax.rsqrt(var + GN_EPS) * lnw_ref[...] + lnb_ref[...] + bonus_ref[...]) * (za * _sigmoid(za))
    ob = jnp.where(is_prompt, ob_p_ref[...], ob_s_ref[...])
    zb = zb_ref[...]
    hg = hg_ref[...]
    parts = []
    for g in range(ob.shape[1] // HEAD_B):
        og = ob[:, g * HEAD_B:(g + 1) * HEAD_B]
        ms = jnp.mean(og * og, axis=-1, keepdims=True)
        parts.append(og * lax.rsqrt(ms + NORM_EPS) * hg)
    yb = jnp.concatenate(parts, axis=1) * (zb * _sigmoid(zb))
    merged = (_sigmoid(ga_ref[...]) * jnp.dot(ya.astype(BF16), pa_ref[...], preferred_element_type=F32)
              + _sigmoid(gb_ref[...]) * jnp.dot(yb.astype(BF16), pb_ref[...], preferred_element_type=F32))
    y_ref[...] = x_ref[...] + jnp.dot(merged.astype(BF16), wo_ref[...], preferred_element_type=F32)


def _out_proj(x, yrec_p, yrec_s, bonus, pb, ob_p, ob_s, ln_w, ln_b, hg_g, pat, proj_a, proj_b, w_out, layer,
              *, tm, col_za, col_zb, col_ga, col_gb):
    t, d = x.shape
    wa_ = yrec_p.shape[1]
    assert yrec_p.shape[0] % tm == 0 and yrec_s.shape[0] % tm == 0
    prompt_tiles = yrec_p.shape[0] // tm
    row = lambda i: (i, 0)
    fixed = lambda i: (0, 0)
    half = lambda cb: pl.BlockSpec((tm, wa_), lambda i: (i, cb))
    full = lambda cb: pl.BlockSpec((tm, d), lambda i: (i, cb))
    first = pl.BlockSpec((tm, wa_), lambda i: (jnp.minimum(i, prompt_tiles - 1), 0))
    second = pl.BlockSpec((tm, wa_), lambda i: (jnp.maximum(i - prompt_tiles, 0), 0))
    const = lambda a: pl.BlockSpec(a.shape, fixed, pipeline_mode=pl.Buffered(1))
    weight = lambda a: pl.BlockSpec((None,) + a.shape[1:], lambda i: (layer, 0, 0),
                                    pipeline_mode=pl.Buffered(1))
    return pl.pallas_call(
        functools.partial(_out_proj_body, prompt_tiles=prompt_tiles),
        grid=(t // tm,),
        in_specs=[
            pl.BlockSpec((tm, d), row), first, second, half(0), half(col_za), first, second, half(col_zb),
            full(col_ga), full(col_gb),
            const(ln_w), const(ln_b), const(hg_g), const(pat),
            weight(proj_a), weight(proj_b), weight(w_out),
        ],
        out_specs=pl.BlockSpec((tm, d), row),
        out_shape=jax.ShapeDtypeStruct((t, d), F32),
        compiler_params=_params("parallel"),
        name="out_proj",
    )(x, yrec_p, yrec_s, bonus, pb, ob_p, ob_s, pb, pb, pb, ln_w, ln_b, hg_g, pat, proj_a, proj_b, w_out)


def _final_norm_body(x_ref, g_ref, o_ref):
    x = x_ref[...]
    ms = jnp.mean(x * x, axis=-1, keepdims=True)
    o_ref[...] = x * lax.rsqrt(ms + NORM_EPS) * g_ref[...]


def _final_norm(x, g, tm):
    t, d = x.shape
    return pl.pallas_call(
        _final_norm_body,
        grid=(t // tm,),
        in_specs=[pl.BlockSpec((tm, d), lambda i: (i, 0)), pl.BlockSpec((1, d), lambda i: (0, 0))],
        out_specs=pl.BlockSpec((tm, d), lambda i: (i, 0)),
        out_shape=jax.ShapeDtypeStruct((t, d), F32),
        compiler_params=_params("parallel"),
        name="final_norm",
    )(x, g)


def _perm_key(a, heads):
    lead = a.shape[:-1]
    return jnp.swapaxes(a.reshape(*lead, heads, HEAD_A), -1, -2).reshape(*lead, heads * HEAD_A)


def _unperm_key(a, heads):
    lead = a.shape[:-1]
    return jnp.swapaxes(a.reshape(*lead, HEAD_A, heads), -1, -2).reshape(*lead, heads * HEAD_A)


def _perm_val(a, heads, dup):
    lead = a.shape[:-1]
    x = a.reshape(*lead, heads, dup, HEAD_A // dup)
    return jnp.moveaxis(x, (-3, -2, -1), (-1, -2, -3)).reshape(*lead, heads * HEAD_A)


def _unperm_val(a, heads, dup):
    lead = a.shape[:-1]
    x = a.reshape(*lead, HEAD_A // dup, dup, heads)
    return jnp.moveaxis(x, (-3, -2, -1), (-1, -2, -3)).reshape(*lead, heads * HEAD_A)


def _perm_shift(a, heads, dup):
    w = heads * HEAD_A
    return jnp.concatenate([_perm_key(a[..., :w], heads), _perm_key(a[..., w:2 * w], heads),
                            _perm_val(a[..., 2 * w:3 * w], heads, dup), a[..., 3 * w:]], axis=-1)


def _unperm_shift(a, heads, dup):
    w = heads * HEAD_A
    return jnp.concatenate([_unperm_key(a[..., :w], heads), _unperm_key(a[..., w:2 * w], heads),
                            _unperm_val(a[..., 2 * w:3 * w], heads, dup), a[..., 3 * w:]], axis=-1)


def kernel(x_prompt, x_sample, state_rwkv, state_hgrn, state_shift, norm_g, w_in, shift_mu, rwkv_w0, rwkv_w2, rwkv_a0, rwkv_a2, rwkv_k_k, rwkv_k_a, rwkv_r_k, rwkv_ln_w, rwkv_ln_b, hgrn_lb_logits, hgrn_norm_g, proj_a, proj_b, w_out, final_norm_g):
    nb, seq, d = x_prompt.shape
    db, dseq, _ = x_sample.shape
    depth = w_in.shape[0]
    wa_ = rwkv_w0.shape[1]
    wb_ = hgrn_lb_logits.shape[1]
    heads_a = wa_ // HEAD_A
    heads_b = wb_ // HEAD_B
    sw = 3 * wa_ + 2 * LORA
    assert w_in.shape[2] == sw + wa_ + 4 * wb_ + 2 * d and wa_ == wb_ and d == 2 * wa_
    dup = V7X_LANES // (nb * heads_a)
    assert nb * heads_a * dup == V7X_LANES and (db * heads_a) % V7X_LANES == 0
    vl = HEAD_A // dup
    tp, ts = nb * seq, db * dseq
    t_all = tp + ts
    pk = functools.partial(_perm_key, heads=heads_a)
    pv = functools.partial(_perm_val, heads=heads_a, dup=dup)

    off = sw
    za_w = pv(w_in[:, :, off:off + wa_])
    qb_w = w_in[:, :, off + wa_:off + wa_ + wb_]
    fb_w = w_in[:, :, off + wa_ + wb_:off + wa_ + 2 * wb_]
    ib_w = w_in[:, :, off + wa_ + 2 * wb_:off + wa_ + 3 * wb_]
    zb_w = w_in[:, :, off + wa_ + 3 * wb_:off + wa_ + 4 * wb_]
    ga_w = w_in[:, :, off + wa_ + 4 * wb_:off + wa_ + 4 * wb_ + d]
    gb_w = w_in[:, :, off + wa_ + 4 * wb_ + d:]
    w_first = _perm_shift(w_in[:, :, :sw], heads_a, dup).astype(BF16)
    w_second = jnp.concatenate([ga_w, gb_w, za_w, qb_w, fb_w, ib_w, zb_w], axis=2).astype(BF16)
    col_ga, col_gb = 0, 1
    col_za, col_q, col_f, col_i, col_zb = 4, 5, 6, 7, 8
    proj_a_h = (proj_a.reshape(depth, heads_a, dup, vl, d).transpose(0, 3, 2, 1, 4)
                .reshape(depth, wa_, d).astype(BF16))
    proj_b_h, w_out_h = proj_b.astype(BF16), w_out.astype(BF16)
    mu_p = _perm_shift(shift_mu, heads_a, dup)
    w0_p, a0_p, kk_p, ka_p = pk(rwkv_w0), pk(rwkv_a0), pk(rwkv_k_k), pk(rwkv_k_a)
    rk_p = pk(rwkv_r_k.reshape(depth, wa_))
    lnw_p, lnb_p = pv(rwkv_ln_w), pv(rwkv_ln_b)

    zero = jnp.zeros((depth, LORA, wa_), F32)
    lora = jnp.concatenate([jnp.concatenate([pk(rwkv_w2), zero], axis=2),
                            jnp.concatenate([zero, pk(rwkv_a2)], axis=2)], axis=1)
    lora_hi = lora.astype(BF16)
    lora_lo = (lora - lora_hi.astype(F32)).astype(BF16)
    lane_head = jnp.arange(V7X_LANES) % heads_a
    pat = (lane_head[:, None] == lane_head[None, :]).astype(BF16)

    lbs = _lower_bounds(hgrn_lb_logits)

    tm_in = _tile(t_all, 1024)
    tm_pre = _tile(math.gcd(seq, ts), 256)
    tm_out = _tile(math.gcd(tp, ts), 256)
    tb_p = _tile(seq, 32)
    chunk_p = _tile(seq, 64)
    ns_s = _tile(db, 2, 1)

    x = jnp.concatenate([x_prompt.reshape(tp, d), x_sample.reshape(ts, d)], axis=0)
    zero_a = jnp.zeros((HEAD_A, vl, V7X_LANES), F32)
    zero_b = jnp.zeros((1, nb, heads_b, HEAD_B, HEAD_B), F32)
    shift_in = jnp.pad(_perm_shift(state_shift, heads_a, dup)[:, :, None, :],
                       ((0, 0), (0, 0), (0, dseq - 1), (0, 0))).reshape(depth, ts, sw)
    lanes_s = db * heads_a
    outs = [[] for _ in range(6)]
    for l in range(depth):
        row = lambda a: a[l][None]
        p1 = _in_proj(x, row(norm_g), w_first, l, tm_in, _tile(sw, 640, V7X_LANES), "in_proj_shift")
        p2 = _in_proj(x, row(norm_g), w_second, l, tm_in, _tile(w_second.shape[2], 1024, V7X_LANES), "in_proj_rest")

        r, w, k, v, kk, b, bonus = _rwkv_pre(
            p1, shift_in[l], row(mu_p), row(w0_p), row(a0_p), row(kk_p), row(ka_p), row(rk_p),
            lora_hi, lora_lo, l, pat, tm=tm_pre, prompt_rows=tp, seq=seq, sample_len=dseq)

        keyed = [_to_lanes_key(a, nb, seq, heads_a, dup) for a in (kk, w, b, k, r)]
        y_p, sa_p = _rwkv_rec(*keyed, _to_lanes_val(v, nb, seq, heads_a, dup), zero_a, tb_p, True)
        y_p = _from_lanes_val(y_p, nb, heads_a, dup)
        sa_p = (sa_p.reshape(HEAD_A, vl, dup, nb, heads_a).transpose(3, 4, 2, 1, 0)
                .reshape(nb, heads_a, HEAD_A, HEAD_A))
        to_s = lambda a: a[tp:].reshape(db, dseq * wa_).T.reshape(dseq, HEAD_A, lanes_s)
        s0_s = (state_rwkv[l].reshape(db, heads_a, dup, vl, HEAD_A).transpose(4, 3, 2, 1, 0)
                .reshape(HEAD_A, HEAD_A, lanes_s))
        y_s, sa_s = _rwkv_rec(*[to_s(a) for a in (kk, w, b, k, r, v)], s0_s, dseq, False)
        y_s = y_s.reshape(dseq * wa_, db).T.reshape(ts, wa_)
        sa_s = (sa_s.reshape(HEAD_A, vl, dup, heads_a, db).transpose(4, 3, 2, 1, 0)
                .reshape(db, heads_a, HEAD_A, HEAD_A))
        cols = dict(col_q=col_q, col_f=col_f, col_i=col_i)
        o_p, sb_p = _hgrn(p2, row(lbs), zero_b, 0, row_off=0, n_seq_total=nb, length=seq,
                          chunk=chunk_p, n_seq=1, **cols)
        o_s, sb_s = _hgrn(p2, row(lbs), state_hgrn, l, row_off=tp, n_seq_total=db,
                          length=dseq, chunk=dseq, n_seq=ns_s, **cols)
        x = _out_proj(x, y_p, y_s, bonus, p2, o_p, o_s, row(lnw_p), row(lnb_p), row(hgrn_norm_g), pat,
                      proj_a_h, proj_b_h, w_out_h, l,
                      tm=tm_out, col_za=col_za, col_zb=col_zb, col_ga=col_ga, col_gb=col_gb)

        last_p = _unperm_shift(p1[seq - 1:tp:seq], heads_a, dup)
        last_s = _unperm_shift(p1[tp + dseq - 1::dseq], heads_a, dup)
        for dst, val in zip(outs, (sa_p, sb_p, last_p, sa_s, sb_s, last_s)):
            dst.append(val)

    y = _final_norm(x, final_norm_g[None], _tile(t_all, 512))
    return (y[:tp].reshape(nb, seq, d), y[tp:].reshape(db, dseq, d)) + tuple(jnp.stack(o) for o in outs)
```

```python
import functools
import math

import jax
import jax.numpy as jnp
from jax import lax
from jax.experimental import pallas as pl
from jax.experimental.pallas import tpu as pltpu

F32 = jnp.float32
BF16 = jnp.bfloat16

NORM_EPS = 1e-6
GN_EPS = 64e-5
MAX_INPUT_GATE = 1.0 - 1e-6
KK_NORM_FLOOR = 1e-12

HEAD_A = 64
HEAD_B = 128
LORA = 64
V7X_LANES = 128
V7X_VMEM_BYTES = 64 * 1024 * 1024
VMEM_LIMIT = (V7X_VMEM_BYTES * 7) // 8
GLA_SUB = 16
ROW_TILE = 8
RWKV_STEPS_PER_ITER = 8
T_BLOCK = V7X_LANES


def _tile(total, pref, mult=8):
    best = None
    for t in range(mult, min(total, pref) + 1, mult):
        if total % t == 0:
            best = t
    assert best is not None, (total, pref, mult)
    return best


def _params(*sem):
    return pltpu.CompilerParams(dimension_semantics=sem, vmem_limit_bytes=VMEM_LIMIT)


def _sigmoid(x):
    return 1.0 / (1.0 + jnp.exp(-x))


def _head_sum(x, heads):
    groups = x.shape[1] // V7X_LANES
    acc = x[:, :V7X_LANES]
    for g in range(1, groups):
        acc = acc + x[:, g * V7X_LANES:(g + 1) * V7X_LANES]
    shift = heads
    while shift < V7X_LANES:
        acc = acc + pltpu.roll(acc, shift, 1)
        shift *= 2
    return jnp.concatenate([acc] * groups, axis=1)


def _lower_bound_body(logits_ref, lb_ref):
    x = logits_ref[...]
    e = jnp.exp(x - jnp.max(x, axis=0, keepdims=True))
    probs = e / jnp.sum(e, axis=0, keepdims=True)
    acc = jnp.zeros_like(probs[0:1])
    lb_ref[0:1, :] = acc
    for l in range(1, x.shape[0]):
        acc = acc + probs[l:l + 1]
        lb_ref[l:l + 1, :] = acc


def _lower_bounds(logits):
    return pl.pallas_call(
        _lower_bound_body,
        out_shape=jax.ShapeDtypeStruct(logits.shape, F32),
        name="hgrn_lower_bounds",
    )(logits)


def _in_proj_body(x_ref, g_ref, w_ref, o_ref, h_ref):
    @pl.when(pl.program_id(1) == 0)
    def _():
        x = x_ref[...]
        ms = jnp.mean(x * x, axis=-1, keepdims=True)
        h_ref[...] = (x * lax.rsqrt(ms + NORM_EPS) * g_ref[...]).astype(BF16)

    o_ref[...] = jnp.dot(h_ref[...], w_ref[...], preferred_element_type=F32)


def _in_proj(x, g, w, layer, tm, tn, name):
    t, d = x.shape
    n = w.shape[2]
    return pl.pallas_call(
        _in_proj_body,
        grid=(t // tm, n // tn),
        in_specs=[
            pl.BlockSpec((tm, d), lambda i, j: (i, 0)),
            pl.BlockSpec((1, d), lambda i, j: (0, 0)),
            pl.BlockSpec((None, d, tn), lambda i, j: (layer, 0, j)),
        ],
        out_specs=pl.BlockSpec((tm, tn), lambda i, j: (i, j)),
        out_shape=jax.ShapeDtypeStruct((t, n), F32),
        scratch_shapes=[pltpu.VMEM((tm, d), BF16)],
        compiler_params=_params("parallel", "arbitrary"),
        name=name,
    )(x, g, w)


def _rwkv_pre_body(cur_ref, tail_ref, bnd_ref, mu_ref, w0_ref, a0_ref, kk_ref, ka_ref, rk_ref,
                   lora_hi_ref, lora_lo_ref,
                   r_o, w_o, k_o, v_o, kk_o, b_o, bonus_o,
                   *, prompt_tiles, tiles_per_seq, sample_len, heads):
    wa_ = r_o.shape[1]
    cur = cur_ref[...]
    i = pl.program_id(0)
    row = lax.broadcasted_iota(jnp.int32, (cur.shape[0], 1), 0)
    carry = jnp.where(i % tiles_per_seq == 0, 0.0, tail_ref[ROW_TILE - 1:ROW_TILE, :])
    prev = jnp.where(row == 0, carry, pltpu.roll(cur, 1, 0))
    seq_start = jnp.logical_and(i >= prompt_tiles, jnp.bitwise_and(row, sample_len - 1) == 0)
    prev = jnp.where(seq_start, bnd_ref[...], prev)
    sh = cur + (prev - cur) * mu_ref[...]
    r = sh[:, :wa_]
    k = sh[:, wa_:2 * wa_]
    v = sh[:, 2 * wa_:3 * wa_]
    wa = sh[:, 3 * wa_:3 * wa_ + 2 * LORA]
    lane = lax.broadcasted_iota(jnp.int32, wa.shape, 1)
    t = jnp.where(lane < LORA, jnp.tanh(wa), wa)
    t1 = t.astype(BF16)
    t2 = (t - t1.astype(F32)).astype(BF16)
    bh = lora_hi_ref[...]
    bl = lora_lo_ref[...]
    lora = (jnp.dot(t1, bl, preferred_element_type=F32)
            + jnp.dot(t2, bh, preferred_element_type=F32)
            + jnp.dot(t1, bh, preferred_element_type=F32))
    z = -(w0_ref[...] + lora[:, :wa_])
    softplus = jnp.maximum(z, 0.0) + jnp.log1p(jnp.exp(-jnp.abs(z)))
    decay = jnp.exp(-jnp.exp(-softplus - 0.5))
    a = _sigmoid(a0_ref[...] + lora[:, wa_:])
    kk = k * kk_ref[...]
    kk = kk / jnp.maximum(jnp.sqrt(_head_sum(kk * kk, heads)), KK_NORM_FLOOR)
    k2 = k * (1.0 + (a - 1.0) * ka_ref[...])
    r_o[...] = r
    w_o[...] = decay
    k_o[...] = k2
    v_o[...] = v
    kk_o[...] = kk
    b_o[...] = kk * a
    bonus_o[...] = _head_sum(r * k2 * rk_ref[...], heads) * v


def _rwkv_pre(cur, bnd, mu, w0, a0, k_k, k_a, r_k, lora_hi, lora_lo, layer, *, tm, prompt_rows, seq, sample_len):
    t, sw = cur.shape
    wa_ = w0.shape[1]
    assert prompt_rows % tm == 0 and seq % tm == 0 and tm % sample_len == 0 and tm % ROW_TILE == 0
    assert sample_len & (sample_len - 1) == 0
    prompt_tiles = prompt_rows // tm
    row = lambda i: (i, 0)
    fixed = lambda i: (0, 0)
    vec = pl.BlockSpec((1, wa_), fixed)
    lora_spec = pl.BlockSpec((None,) + lora_hi.shape[1:], lambda i: (layer, 0, 0))
    out = jax.ShapeDtypeStruct((t, wa_), F32)
    return pl.pallas_call(
        functools.partial(_rwkv_pre_body, prompt_tiles=prompt_tiles, tiles_per_seq=seq // tm,
                          sample_len=sample_len, heads=wa_ // HEAD_A),
        grid=(t // tm,),
        in_specs=[
            pl.BlockSpec((tm, sw), row),
            pl.BlockSpec((ROW_TILE, sw), lambda i: (jnp.maximum(i * (tm // ROW_TILE) - 1, 0), 0)),
            pl.BlockSpec((tm, sw), lambda i: (jnp.maximum(i - prompt_tiles, 0), 0)),
            pl.BlockSpec((1, sw), fixed),
            vec, vec, vec, vec, vec,
            lora_spec, lora_spec,
        ],
        out_specs=[pl.BlockSpec((tm, wa_), row)] * 7,
        out_shape=[out] * 7,
        compiler_params=_params("parallel"),
        name="rwkv_pre",
    )(cur, cur, bnd, mu, w0, a0, k_k, k_a, r_k, lora_hi, lora_lo)


def _rows_to_lanes(x_refs, z_ref):
    for n, x_ref in enumerate(x_refs):
        for c in range(x_ref.shape[1] // V7X_LANES):
            cols = slice(c * V7X_LANES, (c + 1) * V7X_LANES)
            z_ref[n, cols, :] = x_ref[:, cols].T


def _to_lanes_key_body(*refs, n_seq, heads, dup):
    x_refs, o_ref, z_ref = refs[:n_seq], refs[n_seq], refs[n_seq + 1]
    _rows_to_lanes(x_refs, z_ref)
    for k in range(o_ref.shape[0]):
        rows = slice(k * heads, (k + 1) * heads)
        a = jnp.concatenate([z_ref[n, rows, :] for n in range(n_seq)] * dup, axis=0)
        o_ref[k] = a.T


def _to_lanes_val_body(*refs, n_seq, heads, dup):
    x_refs, o_ref, z_ref = refs[:n_seq], refs[n_seq], refs[n_seq + 1]
    _rows_to_lanes(x_refs, z_ref)
    vl = HEAD_A // dup
    for v in range(vl):
        a = jnp.concatenate([z_ref[n, (v * dup + vh) * heads:(v * dup + vh + 1) * heads, :]
                             for vh in range(dup) for n in range(n_seq)], axis=0)
        o_ref[pl.ds(v, T_BLOCK, stride=vl), :] = a.T


def _from_lanes_val_body(y_ref, o_ref, z_ref, *, n_seq, heads, dup):
    vl = HEAD_A // dup
    for v in range(vl):
        a = y_ref[pl.ds(v, T_BLOCK, stride=vl), :].T
        i = 0
        for vh in range(dup):
            for n in range(n_seq):
                z_ref[n, (v * dup + vh) * heads:(v * dup + vh + 1) * heads, :] = a[i * heads:(i + 1) * heads, :]
                i += 1
    for n in range(n_seq):
        for c in range(o_ref.shape[2] // V7X_LANES):
            cols = slice(c * V7X_LANES, (c + 1) * V7X_LANES)
            o_ref[n, :, cols] = z_ref[n, cols, :].T


def _seq_specs(n_seq, length, width):
    blocks = length // T_BLOCK
    return [pl.BlockSpec((T_BLOCK, width), lambda j, n=n: (n * blocks + j, 0)) for n in range(n_seq)]


def _to_lanes_key(x, n_seq, length, heads, dup):
    width = heads * HEAD_A
    assert n_seq * heads * dup == V7X_LANES and length % T_BLOCK == 0
    return pl.pallas_call(
        functools.partial(_to_lanes_key_body, n_seq=n_seq, heads=heads, dup=dup),
        grid=(length // T_BLOCK,),
        in_specs=_seq_specs(n_seq, length, width),
        out_specs=pl.BlockSpec((HEAD_A, T_BLOCK, V7X_LANES), lambda j: (0, j, 0)),
        out_shape=jax.ShapeDtypeStruct((HEAD_A, length, V7X_LANES), F32),
        scratch_shapes=[pltpu.VMEM((n_seq, width, T_BLOCK), F32)],
        compiler_params=_params("parallel"),
        name="to_lanes_key",
    )(*([x] * n_seq))


def _to_lanes_val(x, n_seq, length, heads, dup):
    width = heads * HEAD_A
    vl = HEAD_A // dup
    assert n_seq * heads * dup == V7X_LANES and length % T_BLOCK == 0
    out = pl.pallas_call(
        functools.partial(_to_lanes_val_body, n_seq=n_seq, heads=heads, dup=dup),
        grid=(length // T_BLOCK,),
        in_specs=_seq_specs(n_seq, length, width),
        out_specs=pl.BlockSpec((T_BLOCK * vl, V7X_LANES), lambda j: (j, 0)),
        out_shape=jax.ShapeDtypeStruct((length * vl, V7X_LANES), F32),
        scratch_shapes=[pltpu.VMEM((n_seq, width, T_BLOCK), F32)],
        compiler_params=_params("parallel"),
        name="to_lanes_val",
    )(*([x] * n_seq))
    return out.reshape(length, vl, V7X_LANES)


def _from_lanes_val(y, n_seq, heads, dup):
    length, vl, _ = y.shape
    width = heads * HEAD_A
    out = pl.pallas_call(
        functools.partial(_from_lanes_val_body, n_seq=n_seq, heads=heads, dup=dup),
        grid=(length // T_BLOCK,),
        in_specs=[pl.BlockSpec((T_BLOCK * vl, V7X_LANES), lambda j: (j, 0))],
        out_specs=pl.BlockSpec((n_seq, T_BLOCK, width), lambda j: (0, j, 0)),
        out_shape=jax.ShapeDtypeStruct((n_seq, length, width), F32),
        scratch_shapes=[pltpu.VMEM((n_seq, width, T_BLOCK), F32)],
        compiler_params=_params("parallel"),
        name="from_lanes_val",
    )(y.reshape(length * vl, V7X_LANES))
    return out.reshape(n_seq * length, width)


def _rwkv_rec_body(kk_ref, w_ref, b_ref, k_ref, r_ref, v_ref, s0_ref, y_ref, s_ref, *, key_major):
    n_key = s_ref.shape[0]

    def key_row(ref, k, t):
        return ref[k, pl.ds(t, 1), :] if key_major else ref[t, k:k + 1, :]

    @pl.when(pl.program_id(1) == 0)
    def _():
        s_ref[...] = s0_ref[...]

    def tree(parts):
        while len(parts) > 1:
            parts = [parts[i] + parts[i + 1] for i in range(0, len(parts), 2)]
        return parts[0]

    def step(t):
        acc = [None] * 4
        for k in range(n_key):
            term = s_ref[k] * key_row(kk_ref, k, t)
            acc[k % 4] = term if acc[k % 4] is None else acc[k % 4] + term
        sa = tree(acc)
        vt = v_ref[t]
        acc = [None] * 4
        for k in range(n_key):
            s_new = (s_ref[k] * key_row(w_ref, k, t)
                     + (vt * key_row(k_ref, k, t) - sa * key_row(b_ref, k, t)))
            s_ref[k] = s_new
            term = s_new * key_row(r_ref, k, t)
            acc[k % 4] = term if acc[k % 4] is None else acc[k % 4] + term
        y_ref[t] = tree(acc)

    def body(it, carry):
        for u in range(RWKV_STEPS_PER_ITER):
            step(it * RWKV_STEPS_PER_ITER + u)
        return carry

    lax.fori_loop(0, v_ref.shape[0] // RWKV_STEPS_PER_ITER, body, 0)


def _rwkv_rec(kk, w, b, k, r, v, s0, tb, key_major):
    length, vl, lanes = v.shape
    n_key = s0.shape[0]
    assert tb % RWKV_STEPS_PER_ITER == 0
    if key_major:
        key_spec = pl.BlockSpec((n_key, tb, V7X_LANES), lambda g, j: (0, j, g))
    else:
        key_spec = pl.BlockSpec((tb, n_key, V7X_LANES), lambda g, j: (j, 0, g))
    val_spec = pl.BlockSpec((tb, vl, V7X_LANES), lambda g, j: (j, 0, g))
    st_spec = pl.BlockSpec((n_key, vl, V7X_LANES), lambda g, j: (0, 0, g))
    return pl.pallas_call(
        functools.partial(_rwkv_rec_body, key_major=key_major),
        grid=(lanes // V7X_LANES, length // tb),
        in_specs=[key_spec] * 5 + [val_spec, st_spec],
        out_specs=[val_spec, st_spec],
        out_shape=[jax.ShapeDtypeStruct(v.shape, F32), jax.ShapeDtypeStruct(s0.shape, F32)],
        compiler_params=_params("parallel", "arbitrary"),
        name="rwkv_rec",
    )(kk, w, b, k, r, v, s0)


def _cumsum_rows(x):
    n = x.shape[0]
    row = lax.broadcasted_iota(jnp.int32, x.shape, 0)
    sh = 1
    while sh < n:
        x = x + jnp.where(row >= sh, pltpu.roll(x, sh, 0), 0.0)
        sh *= 2
    return x


def _mm(a, b, dims):
    return lax.dot_general(a.astype(BF16), b.astype(BF16), (dims, ((), ())),
                           preferred_element_type=F32)


def _mm_state(a, st, dims):
    a1 = a.astype(BF16)
    s1 = st.astype(BF16)
    s2 = (st - s1.astype(F32)).astype(BF16)
    dn = (dims, ((), ()))
    dot = functools.partial(lax.dot_general, dimension_numbers=dn, preferred_element_type=F32)
    return dot(a1, s2) + dot(a1, s1)


def _gla_chunk(qp, f, iv, lb, st, sub):
    c = qp.shape[0]
    q = qp * _sigmoid(qp)
    kb = jnp.minimum((1.0 - lb) * _sigmoid(-f), MAX_INPUT_GATE)
    b = _cumsum_rows(jnp.log1p(-kb))
    o_inter = _mm_state(q * jnp.exp(b), st, ((1,), (1,)))
    outs = []
    for i in range(c // sub):
        lo = i * sub
        qi = q[lo:lo + sub]
        bi = b[lo:lo + sub]
        oi = o_inter[lo:lo + sub]
        tiles = []
        for r0 in range(0, sub, ROW_TILE):
            rt = min(ROW_TILE, sub - r0)
            q_t, b_t, o_t = qi[r0:r0 + rt], bi[r0:r0 + rt], oi[r0:r0 + rt]
            row = lax.broadcasted_iota(jnp.int32, (rt, 1), 0) + r0
            for s in range(min(sub, r0 + rt)):
                bs = b[lo + s:lo + s + 1]
                e = jnp.exp(jnp.minimum(b_t - bs, 0.0))
                att = jnp.sum(q_t * e * kb[lo + s:lo + s + 1], axis=-1, keepdims=True)
                if s > r0:
                    att = jnp.where(row >= s, att, 0.0)
                o_t = o_t + att * iv[lo + s:lo + s + 1]
            tiles.append(o_t)
        oi = tiles[0] if len(tiles) == 1 else jnp.concatenate(tiles, axis=0)
        if i > 0:
            b_edge = b[lo - 1:lo]
            qt = qi * jnp.exp(bi - b_edge)
            kh = kb[:lo] * jnp.exp(b_edge - b[:lo])
            att = _mm(qt, kh, ((1,), (1,)))
            oi = oi + _mm(att, iv[:lo], ((1,), (0,)))
        outs.append(oi)
    b_last = b[c - 1:c]
    kt = kb * jnp.exp(b_last - b)
    st_new = st * jnp.exp(b_last) + _mm(iv, kt, ((0,), (0,)))
    return jnp.concatenate(outs, axis=0), st_new


def _hgrn_body(q_ref, f_ref, i_ref, lb_ref, s0_ref, o_ref, s_ref, st_ref, *, n_seq, chunk, sub):
    n_heads = q_ref.shape[1] // HEAD_B
    pairs = [(sq, h) for h in range(n_heads) for sq in range(n_seq)]

    @pl.when(pl.program_id(1) == 0)
    def _():
        for sq, h in pairs:
            st_ref[sq, h] = s0_ref[sq, h].T

    for sq, h in pairs:
        col = slice(h * HEAD_B, (h + 1) * HEAD_B)
        rows = slice(sq * chunk, (sq + 1) * chunk)
        o, st = _gla_chunk(q_ref[rows, col], f_ref[rows, col], i_ref[rows, col], lb_ref[:, col],
                           st_ref[sq, h], sub)
        o_ref[rows, col] = o
        st_ref[sq, h] = st

    @pl.when(pl.program_id(1) == pl.num_programs(1) - 1)
    def _():
        for sq, h in pairs:
            s_ref[sq, h] = st_ref[sq, h].T


def _hgrn(pb, lb, s0, layer, *, row_off, n_seq_total, length, chunk, n_seq, col_q, col_f, col_i):
    wb = lb.shape[1]
    n_heads = wb // HEAD_B
    rows = n_seq * chunk
    n_chunks = length // chunk
    assert n_seq == 1 or n_chunks == 1
    assert row_off % rows == 0 and n_seq_total % n_seq == 0
    base = row_off // rows

    def tok(cb):
        return pl.BlockSpec((rows, wb), lambda g, c: (base + g * n_chunks + c, cb))

    st_block = (n_seq, n_heads, HEAD_B, HEAD_B)
    return pl.pallas_call(
        functools.partial(_hgrn_body, n_seq=n_seq, chunk=chunk, sub=min(GLA_SUB, chunk)),
        grid=(n_seq_total // n_seq, n_chunks),
        in_specs=[tok(col_q), tok(col_f), tok(col_i), pl.BlockSpec((1, wb), lambda g, c: (0, 0)),
                  pl.BlockSpec((None,) + st_block, lambda g, c: (layer, g, 0, 0, 0))],
        out_specs=[pl.BlockSpec((rows, wb), lambda g, c: (g * n_chunks + c, 0)),
                   pl.BlockSpec(st_block, lambda g, c: (g, 0, 0, 0))],
        out_shape=[jax.ShapeDtypeStruct((n_seq_total * length, wb), F32),
                   jax.ShapeDtypeStruct(s0.shape[1:], F32)],
        scratch_shapes=[pltpu.VMEM(st_block, F32)],
        compiler_params=_params("parallel", "arbitrary"),
        name="hgrn",
    )(pb, pb, pb, lb, s0)


def _out_proj_body(x_ref, yrec_p_ref, yrec_s_ref, bonus_ref, za_ref, ob_p_ref, ob_s_ref, zb_ref, ga_ref, gb_ref,
                   lnw_ref, lnb_ref, hg_ref, pa_ref, pb_ref, wo_ref, y_ref, *, prompt_tiles):
    heads = yrec_p_ref.shape[1] // HEAD_A
    inv_head = 1.0 / HEAD_A
    is_prompt = pl.program_id(0) < prompt_tiles
    y = jnp.where(is_prompt, yrec_p_ref[...], yrec_s_ref[...])
    mu = _head_sum(y, heads) * inv_head
    yc = y - mu
    var = _head_sum(yc * yc, heads) * inv_head
    za = za_ref[...]
    ya = (yc * lax.rsqrt(var + GN_EPS) * lnw_ref[...] + lnb_ref[...] + bonus_ref[...]) * (za * _sigmoid(za))
    ob = jnp.where(is_prompt, ob_p_ref[...], ob_s_ref[...])
    zb = zb_ref[...]
    hg = hg_ref[...]
    parts = []
    for g in range(ob.shape[1] // HEAD_B):
        og = ob[:, g * HEAD_B:(g + 1) * HEAD_B]
        ms = jnp.mean(og * og, axis=-1, keepdims=True)
        parts.append(og * lax.rsqrt(ms + NORM_EPS) * hg)
    yb = jnp.concatenate(parts, axis=1) * (zb * _sigmoid(zb))
    merged = (_sigmoid(ga_ref[...]) * jnp.dot(ya.astype(BF16), pa_ref[...], preferred_element_type=F32)
              + _sigmoid(gb_ref[...]) * jnp.dot(yb.astype(BF16), pb_ref[...], preferred_element_type=F32))
    y_ref[...] = x_ref[...] + jnp.dot(merged.astype(BF16), wo_ref[...], preferred_element_type=F32)


def _out_proj(x, yrec_p, yrec_s, bonus, pb, ob_p, ob_s, ln_w, ln_b, hg_g, proj_a, proj_b, w_out, layer,
              *, tm, col_za, col_zb, col_ga, col_gb):
    t, d = x.shape
    wa_ = yrec_p.shape[1]
    assert yrec_p.shape[0] % tm == 0 and yrec_s.shape[0] % tm == 0
    prompt_tiles = yrec_p.shape[0] // tm
    row = lambda i: (i, 0)
    fixed = lambda i: (0, 0)
    half = lambda cb: pl.BlockSpec((tm, wa_), lambda i: (i, cb))
    full = lambda cb: pl.BlockSpec((tm, d), lambda i: (i, cb))
    first = pl.BlockSpec((tm, wa_), lambda i: (jnp.minimum(i, prompt_tiles - 1), 0))
    second = pl.BlockSpec((tm, wa_), lambda i: (jnp.maximum(i - prompt_tiles, 0), 0))
    const = lambda a: pl.BlockSpec(a.shape, fixed, pipeline_mode=pl.Buffered(1))
    weight = lambda a: pl.BlockSpec((None,) + a.shape[1:], lambda i: (layer, 0, 0),
                                    pipeline_mode=pl.Buffered(1))
    return pl.pallas_call(
        functools.partial(_out_proj_body, prompt_tiles=prompt_tiles),
        grid=(t // tm,),
        in_specs=[
            pl.BlockSpec((tm, d), row), first, second, half(0), half(col_za), first, second, half(col_zb),
            full(col_ga), full(col_gb),
            const(ln_w), const(ln_b), const(hg_g),
            weight(proj_a), weight(proj_b), weight(w_out),
        ],
        out_specs=pl.BlockSpec((tm, d), row),
        out_shape=jax.ShapeDtypeStruct((t, d), F32),
        compiler_params=_params("parallel"),
        name="out_proj",
    )(x, yrec_p, yrec_s, bonus, pb, ob_p, ob_s, pb, pb, pb, ln_w, ln_b, hg_g, proj_a, proj_b, w_out)


def _final_norm_body(x_ref, g_ref, o_ref):
    x = x_ref[...]
    ms = jnp.mean(x * x, axis=-1, keepdims=True)
    o_ref[...] = x * lax.rsqrt(ms + NORM_EPS) * g_ref[...]


def _final_norm(x, g, row_off, rows, tm):
    d = x.shape[1]
    assert row_off % tm == 0 and rows % tm == 0
    first = row_off // tm
    return pl.pallas_call(
        _final_norm_body,
        grid=(rows // tm,),
        in_specs=[pl.BlockSpec((tm, d), lambda i: (first + i, 0)), pl.BlockSpec((1, d), lambda i: (0, 0))],
        out_specs=pl.BlockSpec((tm, d), lambda i: (i, 0)),
        out_shape=jax.ShapeDtypeStruct((rows, d), F32),
        compiler_params=_params("parallel"),
        name="final_norm",
    )(x, g)


def _perm_key(a, heads):
    lead = a.shape[:-1]
    return jnp.swapaxes(a.reshape(*lead, heads, HEAD_A), -1, -2).reshape(*lead, heads * HEAD_A)


def _unperm_key(a, heads):
    lead = a.shape[:-1]
    return jnp.swapaxes(a.reshape(*lead, HEAD_A, heads), -1, -2).reshape(*lead, heads * HEAD_A)


def _perm_val(a, heads, dup):
    lead = a.shape[:-1]
    x = a.reshape(*lead, heads, dup, HEAD_A // dup)
    return jnp.moveaxis(x, (-3, -2, -1), (-1, -2, -3)).reshape(*lead, heads * HEAD_A)


def _unperm_val(a, heads, dup):
    lead = a.shape[:-1]
    x = a.reshape(*lead, HEAD_A // dup, dup, heads)
    return jnp.moveaxis(x, (-3, -2, -1), (-1, -2, -3)).reshape(*lead, heads * HEAD_A)


def _perm_shift(a, heads, dup):
    w = heads * HEAD_A
    return jnp.concatenate([_perm_key(a[..., :w], heads), _perm_key(a[..., w:2 * w], heads),
                            _perm_val(a[..., 2 * w:3 * w], heads, dup), a[..., 3 * w:]], axis=-1)


def _unperm_shift(a, heads, dup):
    w = heads * HEAD_A
    return jnp.concatenate([_unperm_key(a[..., :w], heads), _unperm_key(a[..., w:2 * w], heads),
                            _unperm_val(a[..., 2 * w:3 * w], heads, dup), a[..., 3 * w:]], axis=-1)


def kernel(x_prompt, x_sample, state_rwkv, state_hgrn, state_shift, norm_g, w_in, shift_mu, rwkv_w0, rwkv_w2, rwkv_a0, rwkv_a2, rwkv_k_k, rwkv_k_a, rwkv_r_k, rwkv_ln_w, rwkv_ln_b, hgrn_lb_logits, hgrn_norm_g, proj_a, proj_b, w_out, final_norm_g):
    nb, seq, d = x_prompt.shape
    db, dseq, _ = x_sample.shape
    depth = w_in.shape[0]
    wa_ = rwkv_w0.shape[1]
    wb_ = hgrn_lb_logits.shape[1]
    heads_a = wa_ // HEAD_A
    heads_b = wb_ // HEAD_B
    sw = 3 * wa_ + 2 * LORA
    assert w_in.shape[2] == sw + wa_ + 4 * wb_ + 2 * d and wa_ == wb_ and d == 2 * wa_
    dup = V7X_LANES // (nb * heads_a)
    assert nb * heads_a * dup == V7X_LANES and (db * heads_a) % V7X_LANES == 0
    vl = HEAD_A // dup
    tp, ts = nb * seq, db * dseq
    t_all = tp + ts
    pk = functools.partial(_perm_key, heads=heads_a)
    pv = functools.partial(_perm_val, heads=heads_a, dup=dup)

    off = sw
    za_w = pv(w_in[:, :, off:off + wa_])
    qb_w = w_in[:, :, off + wa_:off + wa_ + wb_]
    fb_w = w_in[:, :, off + wa_ + wb_:off + wa_ + 2 * wb_]
    ib_w = w_in[:, :, off + wa_ + 2 * wb_:off + wa_ + 3 * wb_]
    zb_w = w_in[:, :, off + wa_ + 3 * wb_:off + wa_ + 4 * wb_]
    ga_w = w_in[:, :, off + wa_ + 4 * wb_:off + wa_ + 4 * wb_ + d]
    gb_w = w_in[:, :, off + wa_ + 4 * wb_ + d:]
    w_first = _perm_shift(w_in[:, :, :sw], heads_a, dup).astype(BF16)
    w_second = jnp.concatenate([ga_w, gb_w, za_w, qb_w, fb_w, ib_w, zb_w], axis=2).astype(BF16)
    col_ga, col_gb = 0, 1
    col_za, col_q, col_f, col_i, col_zb = 4, 5, 6, 7, 8
    proj_a_h = (proj_a.reshape(depth, heads_a, dup, vl, d).transpose(0, 3, 2, 1, 4)
                .reshape(depth, wa_, d).astype(BF16))
    proj_b_h, w_out_h = proj_b.astype(BF16), w_out.astype(BF16)
    mu_p = _perm_shift(shift_mu, heads_a, dup)
    w0_p, a0_p, kk_p, ka_p = pk(rwkv_w0), pk(rwkv_a0), pk(rwkv_k_k), pk(rwkv_k_a)
    rk_p = pk(rwkv_r_k.reshape(depth, wa_))
    lnw_p, lnb_p = pv(rwkv_ln_w), pv(rwkv_ln_b)

    zero = jnp.zeros((depth, LORA, wa_), F32)
    lora = jnp.concatenate([jnp.concatenate([pk(rwkv_w2), zero], axis=2),
                            jnp.concatenate([zero, pk(rwkv_a2)], axis=2)], axis=1)
    lora_hi = lora.astype(BF16)
    lora_lo = (lora - lora_hi.astype(F32)).astype(BF16)

    lbs = _lower_bounds(hgrn_lb_logits)

    tm_in = _tile(t_all, 1024)
    tm_pre = _tile(math.gcd(seq, ts), 256)
    tm_out = _tile(math.gcd(tp, ts), 256)
    tb_p = _tile(seq, 32)
    chunk_p = _tile(seq, 64)
    ns_s = _tile(db, 2, 1)

    x = jnp.concatenate([x_prompt.reshape(tp, d), x_sample.reshape(ts, d)], axis=0)
    zero_a = jnp.zeros((HEAD_A, vl, V7X_LANES), F32)
    zero_b = jnp.zeros((1, nb, heads_b, HEAD_B, HEAD_B), F32)
    shift_in = jnp.pad(_perm_shift(state_shift, heads_a, dup)[:, :, None, :],
                       ((0, 0), (0, 0), (0, dseq - 1), (0, 0))).reshape(depth, ts, sw)
    lanes_s = db * heads_a
    outs = [[] for _ in range(6)]
    for l in range(depth):
        row = lambda a: a[l][None]
        p1 = _in_proj(x, row(norm_g), w_first, l, tm_in, _tile(sw, 640, V7X_LANES), "in_proj_shift")
        p2 = _in_proj(x, row(norm_g), w_second, l, tm_in, _tile(w_second.shape[2], 1024, V7X_LANES), "in_proj_rest")

        r, w, k, v, kk, b, bonus = _rwkv_pre(
            p1, shift_in[l], row(mu_p), row(w0_p), row(a0_p), row(kk_p), row(ka_p), row(rk_p),
            lora_hi, lora_lo, l, tm=tm_pre, prompt_rows=tp, seq=seq, sample_len=dseq)

        keyed = [_to_lanes_key(a, nb, seq, heads_a, dup) for a in (kk, w, b, k, r)]
        y_p, sa_p = _rwkv_rec(*keyed, _to_lanes_val(v, nb, seq, heads_a, dup), zero_a, tb_p, True)
        y_p = _from_lanes_val(y_p, nb, heads_a, dup)
        sa_p = (sa_p.reshape(HEAD_A, vl, dup, nb, heads_a).transpose(3, 4, 2, 1, 0)
                .reshape(nb, heads_a, HEAD_A, HEAD_A))
        to_s = lambda a: a[tp:].reshape(db, dseq * wa_).T.reshape(dseq, HEAD_A, lanes_s)
        s0_s = (state_rwkv[l].reshape(db, heads_a, dup, vl, HEAD_A).transpose(4, 3, 2, 1, 0)
                .reshape(HEAD_A, HEAD_A, lanes_s))
        y_s, sa_s = _rwkv_rec(*[to_s(a) for a in (kk, w, b, k, r, v)], s0_s, dseq, False)
        y_s = y_s.reshape(dseq * wa_, db).T.reshape(ts, wa_)
        sa_s = (sa_s.reshape(HEAD_A, vl, dup, heads_a, db).transpose(4, 3, 2, 1, 0)
                .reshape(db, heads_a, HEAD_A, HEAD_A))
        cols = dict(col_q=col_q, col_f=col_f, col_i=col_i)
        o_p, sb_p = _hgrn(p2, row(lbs), zero_b, 0, row_off=0, n_seq_total=nb, length=seq,
                          chunk=chunk_p, n_seq=1, **cols)
        o_s, sb_s = _hgrn(p2, row(lbs), state_hgrn, l, row_off=tp, n_seq_total=db,
                          length=dseq, chunk=dseq, n_seq=ns_s, **cols)
        x = _out_proj(x, y_p, y_s, bonus, p2, o_p, o_s, row(lnw_p), row(lnb_p), row(hgrn_norm_g),
                      proj_a_h, proj_b_h, w_out_h, l,
                      tm=tm_out, col_za=col_za, col_zb=col_zb, col_ga=col_ga, col_gb=col_gb)

        last_p = _unperm_shift(p1[seq - 1:tp:seq], heads_a, dup)
        last_s = _unperm_shift(p1[tp + dseq - 1::dseq], heads_a, dup)
        for dst, val in zip(outs, (sa_p, sb_p, last_p, sa_s, sb_s, last_s)):
            dst.append(val)

    tm_fin = _tile(math.gcd(tp, ts), 512)
    y_prompt = _final_norm(x, final_norm_g[None], 0, tp, tm_fin).reshape(nb, seq, d)
    y_sample = _final_norm(x, final_norm_g[None], tp, ts, tm_fin).reshape(db, dseq, d)
    return (y_prompt, y_sample) + tuple(jnp.stack(o) for o in outs)
```

```python
import functools
import math

import jax
import jax.numpy as jnp
from jax import lax
from jax.experimental import pallas as pl
from jax.experimental.pallas import tpu as pltpu

F32 = jnp.float32
BF16 = jnp.bfloat16

NORM_EPS = 1e-6
GN_EPS = 64e-5
MAX_INPUT_GATE = 1.0 - 1e-6
KK_NORM_FLOOR = 1e-12

HEAD_A = 64
HEAD_B = 128
LORA = 64
V7X_LANES = 128
V7X_VMEM_BYTES = 64 * 1024 * 1024
VMEM_LIMIT = (V7X_VMEM_BYTES * 7) // 8
GLA_SUB = 16
ROW_TILE = 8
RWKV_STEPS_PER_ITER = 8
REC_BLOCK = 32
T_BLOCK = V7X_LANES


def _tile(total, pref, mult=8):
    best = None
    for t in range(mult, min(total, pref) + 1, mult):
        if total % t == 0:
            best = t
    assert best is not None, (total, pref, mult)
    return best


def _params(*sem):
    return pltpu.CompilerParams(dimension_semantics=sem, vmem_limit_bytes=VMEM_LIMIT)


def _sigmoid(x):
    return 1.0 / (1.0 + jnp.exp(-x))


def _head_sum(x, heads):
    groups = x.shape[1] // V7X_LANES
    acc = x[:, :V7X_LANES]
    for g in range(1, groups):
        acc = acc + x[:, g * V7X_LANES:(g + 1) * V7X_LANES]
    shift = heads
    while shift < V7X_LANES:
        acc = acc + pltpu.roll(acc, shift, 1)
        shift *= 2
    return jnp.concatenate([acc] * groups, axis=1)


def _lower_bound_body(logits_ref, lb_ref):
    x = logits_ref[...]
    e = jnp.exp(x - jnp.max(x, axis=0, keepdims=True))
    probs = e / jnp.sum(e, axis=0, keepdims=True)
    acc = jnp.zeros_like(probs[0:1])
    lb_ref[0:1, :] = acc
    for l in range(1, x.shape[0]):
        acc = acc + probs[l:l + 1]
        lb_ref[l:l + 1, :] = acc


def _lower_bounds(logits):
    return pl.pallas_call(
        _lower_bound_body,
        out_shape=jax.ShapeDtypeStruct(logits.shape, F32),
        name="hgrn_lower_bounds",
    )(logits)


def _in_proj_body(x_ref, g_ref, w_ref, o_ref, h_ref):
    @pl.when(pl.program_id(1) == 0)
    def _():
        x = x_ref[...]
        ms = jnp.mean(x * x, axis=-1, keepdims=True)
        h_ref[...] = (x * lax.rsqrt(ms + NORM_EPS) * g_ref[...]).astype(BF16)

    o_ref[...] = jnp.dot(h_ref[...], w_ref[...], preferred_element_type=F32)


def _in_proj(x, g, w, layer, tm, tn, name):
    t, d = x.shape
    n = w.shape[2]
    return pl.pallas_call(
        _in_proj_body,
        grid=(t // tm, n // tn),
        in_specs=[
            pl.BlockSpec((tm, d), lambda i, j: (i, 0)),
            pl.BlockSpec((1, d), lambda i, j: (0, 0)),
            pl.BlockSpec((None, d, tn), lambda i, j: (layer, 0, j)),
        ],
        out_specs=pl.BlockSpec((tm, tn), lambda i, j: (i, j)),
        out_shape=jax.ShapeDtypeStruct((t, n), F32),
        scratch_shapes=[pltpu.VMEM((tm, d), BF16)],
        compiler_params=_params("parallel", "arbitrary"),
        name=name,
    )(x, g, w)


def _rwkv_pre_body(cur_ref, tail_ref, bnd_ref, mu_ref, w0_ref, a0_ref, kk_ref, ka_ref, rk_ref,
                   lora_hi_ref, lora_lo_ref,
                   r_o, w_o, k_o, v_o, kk_o, b_o, bonus_o,
                   *, prompt_tiles, tiles_per_seq, sample_len, heads):
    wa_ = r_o.shape[1]
    cur = cur_ref[...]
    i = pl.program_id(0)
    row = lax.broadcasted_iota(jnp.int32, (cur.shape[0], 1), 0)
    carry = jnp.where(i % tiles_per_seq == 0, 0.0, tail_ref[ROW_TILE - 1:ROW_TILE, :])
    prev = jnp.where(row == 0, carry, pltpu.roll(cur, 1, 0))
    seq_start = jnp.logical_and(i >= prompt_tiles, jnp.bitwise_and(row, sample_len - 1) == 0)
    prev = jnp.where(seq_start, bnd_ref[...], prev)
    sh = cur + (prev - cur) * mu_ref[...]
    r = sh[:, :wa_]
    k = sh[:, wa_:2 * wa_]
    v = sh[:, 2 * wa_:3 * wa_]
    wa = sh[:, 3 * wa_:3 * wa_ + 2 * LORA]
    lane = lax.broadcasted_iota(jnp.int32, wa.shape, 1)
    t = jnp.where(lane < LORA, jnp.tanh(wa), wa)
    t1 = t.astype(BF16)
    t2 = (t - t1.astype(F32)).astype(BF16)
    bh = lora_hi_ref[...]
    bl = lora_lo_ref[...]
    lora = (jnp.dot(t1, bl, preferred_element_type=F32)
            + jnp.dot(t2, bh, preferred_element_type=F32)
            + jnp.dot(t1, bh, preferred_element_type=F32))
    z = -(w0_ref[...] + lora[:, :wa_])
    softplus = jnp.maximum(z, 0.0) + jnp.log1p(jnp.exp(-jnp.abs(z)))
    log_w = -jnp.exp(-softplus - 0.5)
    a = _sigmoid(a0_ref[...] + lora[:, wa_:])
    kk = k * kk_ref[...]
    kk = kk / jnp.maximum(jnp.sqrt(_head_sum(kk * kk, heads)), KK_NORM_FLOOR)
    k2 = k * (1.0 + (a - 1.0) * ka_ref[...])
    pos = jnp.bitwise_and(row, jnp.where(i >= prompt_tiles, sample_len - 1, REC_BLOCK - 1))
    cl = log_w
    shift = 1
    while shift < max(REC_BLOCK, sample_len):
        cl = cl + jnp.where(pos >= shift, pltpu.roll(cl, shift, 0), 0.0)
        shift *= 2
    grow = jnp.exp(-cl)
    decayed = jnp.exp(cl)
    r_o[...] = r * decayed
    w_o[...] = decayed
    k_o[...] = k2 * grow
    v_o[...] = v
    kk_o[...] = kk * jnp.exp(cl - log_w)
    b_o[...] = kk * a * grow
    bonus_o[...] = _head_sum(r * k2 * rk_ref[...], heads) * v


def _rwkv_pre(cur, bnd, mu, w0, a0, k_k, k_a, r_k, lora_hi, lora_lo, layer, *, tm, prompt_rows, seq, sample_len):
    t, sw = cur.shape
    wa_ = w0.shape[1]
    assert prompt_rows % tm == 0 and seq % tm == 0 and tm % sample_len == 0 and tm % ROW_TILE == 0
    assert sample_len & (sample_len - 1) == 0 and tm % REC_BLOCK == 0
    prompt_tiles = prompt_rows // tm
    row = lambda i: (i, 0)
    fixed = lambda i: (0, 0)
    vec = pl.BlockSpec((1, wa_), fixed)
    lora_spec = pl.BlockSpec((None,) + lora_hi.shape[1:], lambda i: (layer, 0, 0))
    out = jax.ShapeDtypeStruct((t, wa_), F32)
    return pl.pallas_call(
        functools.partial(_rwkv_pre_body, prompt_tiles=prompt_tiles, tiles_per_seq=seq // tm,
                          sample_len=sample_len, heads=wa_ // HEAD_A),
        grid=(t // tm,),
        in_specs=[
            pl.BlockSpec((tm, sw), row),
            pl.BlockSpec((ROW_TILE, sw), lambda i: (jnp.maximum(i * (tm // ROW_TILE) - 1, 0), 0)),
            pl.BlockSpec((tm, sw), lambda i: (jnp.maximum(i - prompt_tiles, 0), 0)),
            pl.BlockSpec((1, sw), fixed),
            vec, vec, vec, vec, vec,
            lora_spec, lora_spec,
        ],
        out_specs=[pl.BlockSpec((tm, wa_), row)] * 7,
        out_shape=[out] * 7,
        compiler_params=_params("parallel"),
        name="rwkv_pre",
    )(cur, cur, bnd, mu, w0, a0, k_k, k_a, r_k, lora_hi, lora_lo)


def _rows_to_lanes(x_refs, z_ref):
    for n, x_ref in enumerate(x_refs):
        for c in range(x_ref.shape[1] // V7X_LANES):
            cols = slice(c * V7X_LANES, (c + 1) * V7X_LANES)
            z_ref[n, cols, :] = x_ref[:, cols].T


def _to_lanes_key_body(*refs, n_seq, heads, dup):
    x_refs, o_ref, z_ref = refs[:n_seq], refs[n_seq], refs[n_seq + 1]
    _rows_to_lanes(x_refs, z_ref)
    for k in range(o_ref.shape[0]):
        rows = slice(k * heads, (k + 1) * heads)
        a = jnp.concatenate([z_ref[n, rows, :] for n in range(n_seq)] * dup, axis=0)
        o_ref[k] = a.T


def _to_lanes_val_body(*refs, n_seq, heads, dup):
    x_refs, o_ref, z_ref = refs[:n_seq], refs[n_seq], refs[n_seq + 1]
    _rows_to_lanes(x_refs, z_ref)
    vl = HEAD_A // dup
    for v in range(vl):
        a = jnp.concatenate([z_ref[n, (v * dup + vh) * heads:(v * dup + vh + 1) * heads, :]
                             for vh in range(dup) for n in range(n_seq)], axis=0)
        o_ref[pl.ds(v, T_BLOCK, stride=vl), :] = a.T


def _from_lanes_val_body(y_ref, o_ref, z_ref, *, n_seq, heads, dup):
    vl = HEAD_A // dup
    for v in range(vl):
        a = y_ref[pl.ds(v, T_BLOCK, stride=vl), :].T
        i = 0
        for vh in range(dup):
            for n in range(n_seq):
                z_ref[n, (v * dup + vh) * heads:(v * dup + vh + 1) * heads, :] = a[i * heads:(i + 1) * heads, :]
                i += 1
    for n in range(n_seq):
        for c in range(o_ref.shape[2] // V7X_LANES):
            cols = slice(c * V7X_LANES, (c + 1) * V7X_LANES)
            o_ref[n, :, cols] = z_ref[n, cols, :].T


def _seq_specs(n_seq, length, width):
    blocks = length // T_BLOCK
    return [pl.BlockSpec((T_BLOCK, width), lambda j, n=n: (n * blocks + j, 0)) for n in range(n_seq)]


def _to_lanes_key(x, n_seq, length, heads, dup):
    width = heads * HEAD_A
    assert n_seq * heads * dup == V7X_LANES and length % T_BLOCK == 0
    return pl.pallas_call(
        functools.partial(_to_lanes_key_body, n_seq=n_seq, heads=heads, dup=dup),
        grid=(length // T_BLOCK,),
        in_specs=_seq_specs(n_seq, length, width),
        out_specs=pl.BlockSpec((HEAD_A, T_BLOCK, V7X_LANES), lambda j: (0, j, 0)),
        out_shape=jax.ShapeDtypeStruct((HEAD_A, length, V7X_LANES), F32),
        scratch_shapes=[pltpu.VMEM((n_seq, width, T_BLOCK), F32)],
        compiler_params=_params("parallel"),
        name="to_lanes_key",
    )(*([x] * n_seq))


def _to_lanes_val(x, n_seq, length, heads, dup):
    width = heads * HEAD_A
    vl = HEAD_A // dup
    assert n_seq * heads * dup == V7X_LANES and length % T_BLOCK == 0
    out = pl.pallas_call(
        functools.partial(_to_lanes_val_body, n_seq=n_seq, heads=heads, dup=dup),
        grid=(length // T_BLOCK,),
        in_specs=_seq_specs(n_seq, length, width),
        out_specs=pl.BlockSpec((T_BLOCK * vl, V7X_LANES), lambda j: (j, 0)),
        out_shape=jax.ShapeDtypeStruct((length * vl, V7X_LANES), F32),
        scratch_shapes=[pltpu.VMEM((n_seq, width, T_BLOCK), F32)],
        compiler_params=_params("parallel"),
        name="to_lanes_val",
    )(*([x] * n_seq))
    return out.reshape(length, vl, V7X_LANES)


def _from_lanes_val(y, n_seq, heads, dup):
    length, vl, _ = y.shape
    width = heads * HEAD_A
    out = pl.pallas_call(
        functools.partial(_from_lanes_val_body, n_seq=n_seq, heads=heads, dup=dup),
        grid=(length // T_BLOCK,),
        in_specs=[pl.BlockSpec((T_BLOCK * vl, V7X_LANES), lambda j: (j, 0))],
        out_specs=pl.BlockSpec((n_seq, T_BLOCK, width), lambda j: (0, j, 0)),
        out_shape=jax.ShapeDtypeStruct((n_seq, length, width), F32),
        scratch_shapes=[pltpu.VMEM((n_seq, width, T_BLOCK), F32)],
        compiler_params=_params("parallel"),
        name="from_lanes_val",
    )(y.reshape(length * vl, V7X_LANES))
    return out.reshape(n_seq * length, width)


def _rwkv_rec_body(kk_ref, w_ref, b_ref, k_ref, r_ref, v_ref, s0_ref, y_ref, s_ref, *, key_major):
    n_key = s_ref.shape[0]

    def key_row(ref, k, t):
        return ref[k, pl.ds(t, 1), :] if key_major else ref[t, k:k + 1, :]

    @pl.when(pl.program_id(1) == 0)
    def _():
        s_ref[...] = s0_ref[...]

    def tree(parts):
        while len(parts) > 1:
            parts = [parts[i] + parts[i + 1] for i in range(0, len(parts), 2)]
        return parts[0]

    def step(t):
        acc = [None] * 4
        for k in range(n_key):
            term = s_ref[k] * key_row(kk_ref, k, t)
            acc[k % 4] = term if acc[k % 4] is None else acc[k % 4] + term
        sa = tree(acc)
        vt = v_ref[t]
        acc = [None] * 4
        for k in range(n_key):
            s_new = s_ref[k] + (vt * key_row(k_ref, k, t) - sa * key_row(b_ref, k, t))
            s_ref[k] = s_new
            term = s_new * key_row(r_ref, k, t)
            acc[k % 4] = term if acc[k % 4] is None else acc[k % 4] + term
        y_ref[t] = tree(acc)

    def body(it, carry):
        for u in range(RWKV_STEPS_PER_ITER):
            step(it * RWKV_STEPS_PER_ITER + u)
        return carry

    lax.fori_loop(0, v_ref.shape[0] // RWKV_STEPS_PER_ITER, body, 0)
    last = v_ref.shape[0] - 1
    for k in range(n_key):
        s_ref[k] = s_ref[k] * key_row(w_ref, k, last)


def _rwkv_rec(kk, w, b, k, r, v, s0, tb, key_major):
    length, vl, lanes = v.shape
    n_key = s0.shape[0]
    assert tb % RWKV_STEPS_PER_ITER == 0
    if key_major:
        key_spec = pl.BlockSpec((n_key, tb, V7X_LANES), lambda g, j: (0, j, g))
    else:
        key_spec = pl.BlockSpec((tb, n_key, V7X_LANES), lambda g, j: (j, 0, g))
    val_spec = pl.BlockSpec((tb, vl, V7X_LANES), lambda g, j: (j, 0, g))
    st_spec = pl.BlockSpec((n_key, vl, V7X_LANES), lambda g, j: (0, 0, g))
    return pl.pallas_call(
        functools.partial(_rwkv_rec_body, key_major=key_major),
        grid=(lanes // V7X_LANES, length // tb),
        in_specs=[key_spec] * 5 + [val_spec, st_spec],
        out_specs=[val_spec, st_spec],
        out_shape=[jax.ShapeDtypeStruct(v.shape, F32), jax.ShapeDtypeStruct(s0.shape, F32)],
        compiler_params=_params("parallel", "arbitrary"),
        name="rwkv_rec",
    )(kk, w, b, k, r, v, s0)


def _cumsum_rows(x):
    n = x.shape[0]
    row = lax.broadcasted_iota(jnp.int32, x.shape, 0)
    sh = 1
    while sh < n:
        x = x + jnp.where(row >= sh, pltpu.roll(x, sh, 0), 0.0)
        sh *= 2
    return x


def _mm(a, b, dims):
    return lax.dot_general(a.astype(BF16), b.astype(BF16), (dims, ((), ())),
                           preferred_element_type=F32)


def _mm_state(a, st, dims):
    a1 = a.astype(BF16)
    s1 = st.astype(BF16)
    s2 = (st - s1.astype(F32)).astype(BF16)
    dn = (dims, ((), ()))
    dot = functools.partial(lax.dot_general, dimension_numbers=dn, preferred_element_type=F32)
    return dot(a1, s2) + dot(a1, s1)


def _gla_chunk(qp, f, iv, lb, st, sub):
    c = qp.shape[0]
    q = qp * _sigmoid(qp)
    kb = jnp.minimum((1.0 - lb) * _sigmoid(-f), MAX_INPUT_GATE)
    b = _cumsum_rows(jnp.log1p(-kb))
    o_inter = _mm_state(q * jnp.exp(b), st, ((1,), (1,)))
    outs = []
    for i in range(c // sub):
        lo = i * sub
        qi = q[lo:lo + sub]
        bi = b[lo:lo + sub]
        oi = o_inter[lo:lo + sub]
        tiles = []
        for r0 in range(0, sub, ROW_TILE):
            rt = min(ROW_TILE, sub - r0)
            q_t, b_t, o_t = qi[r0:r0 + rt], bi[r0:r0 + rt], oi[r0:r0 + rt]
            row = lax.broadcasted_iota(jnp.int32, (rt, 1), 0) + r0
            for s in range(min(sub, r0 + rt)):
                bs = b[lo + s:lo + s + 1]
                e = jnp.exp(jnp.minimum(b_t - bs, 0.0))
                att = jnp.sum(q_t * e * kb[lo + s:lo + s + 1], axis=-1, keepdims=True)
                if s > r0:
                    att = jnp.where(row >= s, att, 0.0)
                o_t = o_t + att * iv[lo + s:lo + s + 1]
            tiles.append(o_t)
        oi = tiles[0] if len(tiles) == 1 else jnp.concatenate(tiles, axis=0)
        if i > 0:
            b_edge = b[lo - 1:lo]
            qt = qi * jnp.exp(bi - b_edge)
            kh = kb[:lo] * jnp.exp(b_edge - b[:lo])
            att = _mm(qt, kh, ((1,), (1,)))
            oi = oi + _mm(att, iv[:lo], ((1,), (0,)))
        outs.append(oi)
    b_last = b[c - 1:c]
    kt = kb * jnp.exp(b_last - b)
    st_new = st * jnp.exp(b_last) + _mm(iv, kt, ((0,), (0,)))
    return jnp.concatenate(outs, axis=0), st_new


def _hgrn_body(q_ref, f_ref, i_ref, lb_ref, s0_ref, o_ref, s_ref, st_ref, *, n_seq, chunk, sub):
    n_heads = q_ref.shape[1] // HEAD_B
    pairs = [(sq, h) for h in range(n_heads) for sq in range(n_seq)]

    @pl.when(pl.program_id(1) == 0)
    def _():
        for sq, h in pairs:
            st_ref[sq, h] = s0_ref[sq, h].T

    for sq, h in pairs:
        col = slice(h * HEAD_B, (h + 1) * HEAD_B)
        rows = slice(sq * chunk, (sq + 1) * chunk)
        o, st = _gla_chunk(q_ref[rows, col], f_ref[rows, col], i_ref[rows, col], lb_ref[:, col],
                           st_ref[sq, h], sub)
        o_ref[rows, col] = o
        st_ref[sq, h] = st

    @pl.when(pl.program_id(1) == pl.num_programs(1) - 1)
    def _():
        for sq, h in pairs:
            s_ref[sq, h] = st_ref[sq, h].T


def _hgrn(pb, lb, s0, layer, *, row_off, n_seq_total, length, chunk, n_seq, col_q, col_f, col_i):
    wb = lb.shape[1]
    n_heads = wb // HEAD_B
    rows = n_seq * chunk
    n_chunks = length // chunk
    assert n_seq == 1 or n_chunks == 1
    assert row_off % rows == 0 and n_seq_total % n_seq == 0
    base = row_off // rows

    def tok(cb):
        return pl.BlockSpec((rows, wb), lambda g, c: (base + g * n_chunks + c, cb))

    st_block = (n_seq, n_heads, HEAD_B, HEAD_B)
    return pl.pallas_call(
        functools.partial(_hgrn_body, n_seq=n_seq, chunk=chunk, sub=min(GLA_SUB, chunk)),
        grid=(n_seq_total // n_seq, n_chunks),
        in_specs=[tok(col_q), tok(col_f), tok(col_i), pl.BlockSpec((1, wb), lambda g, c: (0, 0)),
                  pl.BlockSpec((None,) + st_block, lambda g, c: (layer, g, 0, 0, 0))],
        out_specs=[pl.BlockSpec((rows, wb), lambda g, c: (g * n_chunks + c, 0)),
                   pl.BlockSpec(st_block, lambda g, c: (g, 0, 0, 0))],
        out_shape=[jax.ShapeDtypeStruct((n_seq_total * length, wb), F32),
                   jax.ShapeDtypeStruct(s0.shape[1:], F32)],
        scratch_shapes=[pltpu.VMEM(st_block, F32)],
        compiler_params=_params("parallel", "arbitrary"),
        name="hgrn",
    )(pb, pb, pb, lb, s0)


def _out_proj_body(x_ref, yrec_p_ref, yrec_s_ref, bonus_ref, za_ref, ob_p_ref, ob_s_ref, zb_ref, ga_ref, gb_ref,
                   lnw_ref, lnb_ref, hg_ref, pa_ref, pb_ref, wo_ref, y_ref, *, prompt_tiles):
    heads = yrec_p_ref.shape[1] // HEAD_A
    inv_head = 1.0 / HEAD_A
    is_prompt = pl.program_id(0) < prompt_tiles
    y = jnp.where(is_prompt, yrec_p_ref[...], yrec_s_ref[...])
    mu = _head_sum(y, heads) * inv_head
    yc = y - mu
    var = _head_sum(yc * yc, heads) * inv_head
    za = za_ref[...]
    ya = (yc * lax.rsqrt(var + GN_EPS) * lnw_ref[...] + lnb_ref[...] + bonus_ref[...]) * (za * _sigmoid(za))
    ob = jnp.where(is_prompt, ob_p_ref[...], ob_s_ref[...])
    zb = zb_ref[...]
    hg = hg_ref[...]
    parts = []
    for g in range(ob.shape[1] // HEAD_B):
        og = ob[:, g * HEAD_B:(g + 1) * HEAD_B]
        ms = jnp.mean(og * og, axis=-1, keepdims=True)
        parts.append(og * lax.rsqrt(ms + NORM_EPS) * hg)
    yb = jnp.concatenate(parts, axis=1) * (zb * _sigmoid(zb))
    merged = (_sigmoid(ga_ref[...]) * jnp.dot(ya.astype(BF16), pa_ref[...], preferred_element_type=F32)
              + _sigmoid(gb_ref[...]) * jnp.dot(yb.astype(BF16), pb_ref[...], preferred_element_type=F32))
    y_ref[...] = x_ref[...] + jnp.dot(merged.astype(BF16), wo_ref[...], preferred_element_type=F32)


def _out_proj(x, yrec_p, yrec_s, bonus, pb, ob_p, ob_s, ln_w, ln_b, hg_g, proj_a, proj_b, w_out, layer,
              *, tm, col_za, col_zb, col_ga, col_gb):
    t, d = x.shape
    wa_ = yrec_p.shape[1]
    assert yrec_p.shape[0] % tm == 0 and yrec_s.shape[0] % tm == 0
    prompt_tiles = yrec_p.shape[0] // tm
    row = lambda i: (i, 0)
    fixed = lambda i: (0, 0)
    half = lambda cb: pl.BlockSpec((tm, wa_), lambda i: (i, cb))
    full = lambda cb: pl.BlockSpec((tm, d), lambda i: (i, cb))
    first = pl.BlockSpec((tm, wa_), lambda i: (jnp.minimum(i, prompt_tiles - 1), 0))
    second = pl.BlockSpec((tm, wa_), lambda i: (jnp.maximum(i - prompt_tiles, 0), 0))
    const = lambda a: pl.BlockSpec(a.shape, fixed, pipeline_mode=pl.Buffered(1))
    weight = lambda a: pl.BlockSpec((None,) + a.shape[1:], lambda i: (layer, 0, 0),
                                    pipeline_mode=pl.Buffered(1))
    return pl.pallas_call(
        functools.partial(_out_proj_body, prompt_tiles=prompt_tiles),
        grid=(t // tm,),
        in_specs=[
            pl.BlockSpec((tm, d), row), first, second, half(0), half(col_za), first, second, half(col_zb),
            full(col_ga), full(col_gb),
            const(ln_w), const(ln_b), const(hg_g),
            weight(proj_a), weight(proj_b), weight(w_out),
        ],
        out_specs=pl.BlockSpec((tm, d), row),
        out_shape=jax.ShapeDtypeStruct((t, d), F32),
        compiler_params=_params("parallel"),
        name="out_proj",
    )(x, yrec_p, yrec_s, bonus, pb, ob_p, ob_s, pb, pb, pb, ln_w, ln_b, hg_g, proj_a, proj_b, w_out)


def _final_norm_body(x_ref, g_ref, o_ref):
    x = x_ref[...]
    ms = jnp.mean(x * x, axis=-1, keepdims=True)
    o_ref[...] = x * lax.rsqrt(ms + NORM_EPS) * g_ref[...]


def _final_norm(x, g, row_off, rows, tm):
    d = x.shape[1]
    assert row_off % tm == 0 and rows % tm == 0
    first = row_off // tm
    return pl.pallas_call(
        _final_norm_body,
        grid=(rows // tm,),
        in_specs=[pl.BlockSpec((tm, d), lambda i: (first + i, 0)), pl.BlockSpec((1, d), lambda i: (0, 0))],
        out_specs=pl.BlockSpec((tm, d), lambda i: (i, 0)),
        out_shape=jax.ShapeDtypeStruct((rows, d), F32),
        compiler_params=_params("parallel"),
        name="final_norm",
    )(x, g)


def _perm_key(a, heads):
    lead = a.shape[:-1]
    return jnp.swapaxes(a.reshape(*lead, heads, HEAD_A), -1, -2).reshape(*lead, heads * HEAD_A)


def _unperm_key(a, heads):
    lead = a.shape[:-1]
    return jnp.swapaxes(a.reshape(*lead, HEAD_A, heads), -1, -2).reshape(*lead, heads * HEAD_A)


def _perm_val(a, heads, dup):
    lead = a.shape[:-1]
    x = a.reshape(*lead, heads, dup, HEAD_A // dup)
    return jnp.moveaxis(x, (-3, -2, -1), (-1, -2, -3)).reshape(*lead, heads * HEAD_A)


def _unperm_val(a, heads, dup):
    lead = a.shape[:-1]
    x = a.reshape(*lead, HEAD_A // dup, dup, heads)
    return jnp.moveaxis(x, (-3, -2, -1), (-1, -2, -3)).reshape(*lead, heads * HEAD_A)


def _perm_shift(a, heads, dup):
    w = heads * HEAD_A
    return jnp.concatenate([_perm_key(a[..., :w], heads), _perm_key(a[..., w:2 * w], heads),
                            _perm_val(a[..., 2 * w:3 * w], heads, dup), a[..., 3 * w:]], axis=-1)


def _unperm_shift(a, heads, dup):
    w = heads * HEAD_A
    return jnp.concatenate([_unperm_key(a[..., :w], heads), _unperm_key(a[..., w:2 * w], heads),
                            _unperm_val(a[..., 2 * w:3 * w], heads, dup), a[..., 3 * w:]], axis=-1)


def kernel(x_prompt, x_sample, state_rwkv, state_hgrn, state_shift, norm_g, w_in, shift_mu, rwkv_w0, rwkv_w2, rwkv_a0, rwkv_a2, rwkv_k_k, rwkv_k_a, rwkv_r_k, rwkv_ln_w, rwkv_ln_b, hgrn_lb_logits, hgrn_norm_g, proj_a, proj_b, w_out, final_norm_g):
    nb, seq, d = x_prompt.shape
    db, dseq, _ = x_sample.shape
    depth = w_in.shape[0]
    wa_ = rwkv_w0.shape[1]
    wb_ = hgrn_lb_logits.shape[1]
    heads_a = wa_ // HEAD_A
    heads_b = wb_ // HEAD_B
    sw = 3 * wa_ + 2 * LORA
    assert w_in.shape[2] == sw + wa_ + 4 * wb_ + 2 * d and wa_ == wb_ and d == 2 * wa_
    dup = V7X_LANES // (nb * heads_a)
    assert nb * heads_a * dup == V7X_LANES and (db * heads_a) % V7X_LANES == 0
    vl = HEAD_A // dup
    tp, ts = nb * seq, db * dseq
    t_all = tp + ts
    pk = functools.partial(_perm_key, heads=heads_a)
    pv = functools.partial(_perm_val, heads=heads_a, dup=dup)

    off = sw
    za_w = pv(w_in[:, :, off:off + wa_])
    qb_w = w_in[:, :, off + wa_:off + wa_ + wb_]
    fb_w = w_in[:, :, off + wa_ + wb_:off + wa_ + 2 * wb_]
    ib_w = w_in[:, :, off + wa_ + 2 * wb_:off + wa_ + 3 * wb_]
    zb_w = w_in[:, :, off + wa_ + 3 * wb_:off + wa_ + 4 * wb_]
    ga_w = w_in[:, :, off + wa_ + 4 * wb_:off + wa_ + 4 * wb_ + d]
    gb_w = w_in[:, :, off + wa_ + 4 * wb_ + d:]
    w_first = _perm_shift(w_in[:, :, :sw], heads_a, dup).astype(BF16)
    w_second = jnp.concatenate([ga_w, gb_w, za_w, qb_w, fb_w, ib_w, zb_w], axis=2).astype(BF16)
    col_ga, col_gb = 0, 1
    col_za, col_q, col_f, col_i, col_zb = 4, 5, 6, 7, 8
    proj_a_h = (proj_a.reshape(depth, heads_a, dup, vl, d).transpose(0, 3, 2, 1, 4)
                .reshape(depth, wa_, d).astype(BF16))
    proj_b_h, w_out_h = proj_b.astype(BF16), w_out.astype(BF16)
    mu_p = _perm_shift(shift_mu, heads_a, dup)
    w0_p, a0_p, kk_p, ka_p = pk(rwkv_w0), pk(rwkv_a0), pk(rwkv_k_k), pk(rwkv_k_a)
    rk_p = pk(rwkv_r_k.reshape(depth, wa_))
    lnw_p, lnb_p = pv(rwkv_ln_w), pv(rwkv_ln_b)

    zero = jnp.zeros((depth, LORA, wa_), F32)
    lora = jnp.concatenate([jnp.concatenate([pk(rwkv_w2), zero], axis=2),
                            jnp.concatenate([zero, pk(rwkv_a2)], axis=2)], axis=1)
    lora_hi = lora.astype(BF16)
    lora_lo = (lora - lora_hi.astype(F32)).astype(BF16)

    lbs = _lower_bounds(hgrn_lb_logits)

    tm_in = _tile(t_all, 1024)
    tm_pre = _tile(math.gcd(seq, ts), 256)
    tm_out = _tile(math.gcd(tp, ts), 256)
    assert seq % REC_BLOCK == 0 and dseq <= REC_BLOCK
    tb_p = REC_BLOCK
    chunk_p = _tile(seq, 64)
    ns_s = _tile(db, 2, 1)

    x = jnp.concatenate([x_prompt.reshape(tp, d), x_sample.reshape(ts, d)], axis=0)
    zero_a = jnp.zeros((HEAD_A, vl, V7X_LANES), F32)
    zero_b = jnp.zeros((1, nb, heads_b, HEAD_B, HEAD_B), F32)
    shift_in = jnp.pad(_perm_shift(state_shift, heads_a, dup)[:, :, None, :],
                       ((0, 0), (0, 0), (0, dseq - 1), (0, 0))).reshape(depth, ts, sw)
    lanes_s = db * heads_a
    outs = [[] for _ in range(6)]
    for l in range(depth):
        row = lambda a: a[l][None]
        p1 = _in_proj(x, row(norm_g), w_first, l, tm_in, _tile(sw, 640, V7X_LANES), "in_proj_shift")
        p2 = _in_proj(x, row(norm_g), w_second, l, tm_in, _tile(w_second.shape[2], 1024, V7X_LANES), "in_proj_rest")

        r, w, k, v, kk, b, bonus = _rwkv_pre(
            p1, shift_in[l], row(mu_p), row(w0_p), row(a0_p), row(kk_p), row(ka_p), row(rk_p),
            lora_hi, lora_lo, l, tm=tm_pre, prompt_rows=tp, seq=seq, sample_len=dseq)

        keyed = [_to_lanes_key(a, nb, seq, heads_a, dup) for a in (kk, w, b, k, r)]
        y_p, sa_p = _rwkv_rec(*keyed, _to_lanes_val(v, nb, seq, heads_a, dup), zero_a, tb_p, True)
        y_p = _from_lanes_val(y_p, nb, heads_a, dup)
        sa_p = (sa_p.reshape(HEAD_A, vl, dup, nb, heads_a).transpose(3, 4, 2, 1, 0)
                .reshape(nb, heads_a, HEAD_A, HEAD_A))
        to_s = lambda a: a[tp:].reshape(db, dseq * wa_).T.reshape(dseq, HEAD_A, lanes_s)
        s0_s = (state_rwkv[l].reshape(db, heads_a, dup, vl, HEAD_A).transpose(4, 3, 2, 1, 0)
                .reshape(HEAD_A, HEAD_A, lanes_s))
        y_s, sa_s = _rwkv_rec(*[to_s(a) for a in (kk, w, b, k, r, v)], s0_s, dseq, False)
        y_s = y_s.reshape(dseq * wa_, db).T.reshape(ts, wa_)
        sa_s = (sa_s.reshape(HEAD_A, vl, dup, heads_a, db).transpose(4, 3, 2, 1, 0)
                .reshape(db, heads_a, HEAD_A, HEAD_A))
        cols = dict(col_q=col_q, col_f=col_f, col_i=col_i)
        o_p, sb_p = _hgrn(p2, row(lbs), zero_b, 0, row_off=0, n_seq_total=nb, length=seq,
                          chunk=chunk_p, n_seq=1, **cols)
        o_s, sb_s = _hgrn(p2, row(lbs), state_hgrn, l, row_off=tp, n_seq_total=db,
                          length=dseq, chunk=dseq, n_seq=ns_s, **cols)
        x = _out_proj(x, y_p, y_s, bonus, p2, o_p, o_s, row(lnw_p), row(lnb_p), row(hgrn_norm_g),
                      proj_a_h, proj_b_h, w_out_h, l,
                      tm=tm_out, col_za=col_za, col_zb=col_zb, col_ga=col_ga, col_gb=col_gb)

        last_p = _unperm_shift(p1[seq - 1:tp:seq], heads_a, dup)
        last_s = _unperm_shift(p1[tp + dseq - 1::dseq], heads_a, dup)
        for dst, val in zip(outs, (sa_p, sb_p, last_p, sa_s, sb_s, last_s)):
            dst.append(val)

    tm_fin = _tile(math.gcd(tp, ts), 512)
    y_prompt = _final_norm(x, final_norm_g[None], 0, tp, tm_fin).reshape(nb, seq, d)
    y_sample = _final_norm(x, final_norm_g[None], tp, ts, tm_fin).reshape(db, dseq, d)
    return (y_prompt, y_sample) + tuple(jnp.stack(o) for o in outs)
```

```python
import functools
import math

import jax
import jax.numpy as jnp
from jax import lax
from jax.experimental import pallas as pl
from jax.experimental.pallas import tpu as pltpu

F32 = jnp.float32
BF16 = jnp.bfloat16

NORM_EPS = 1e-6
GN_EPS = 64e-5
MAX_INPUT_GATE = 1.0 - 1e-6
KK_NORM_FLOOR = 1e-12

HEAD_A = 64
HEAD_B = 128
LORA = 64
V7X_LANES = 128
V7X_VMEM_BYTES = 64 * 1024 * 1024
VMEM_LIMIT = (V7X_VMEM_BYTES * 7) // 8
GLA_SUB = 16
ROW_TILE = 8
RWKV_STEPS_PER_ITER = 8
REC_BLOCK = 32
T_BLOCK = V7X_LANES


def _tile(total, pref, mult=8):
    best = None
    for t in range(mult, min(total, pref) + 1, mult):
        if total % t == 0:
            best = t
    assert best is not None, (total, pref, mult)
    return best


def _params(*sem):
    return pltpu.CompilerParams(dimension_semantics=sem, vmem_limit_bytes=VMEM_LIMIT)


def _sigmoid(x):
    return 1.0 / (1.0 + jnp.exp(-x))


def _head_sum(x, heads):
    groups = x.shape[1] // V7X_LANES
    acc = x[:, :V7X_LANES]
    for g in range(1, groups):
        acc = acc + x[:, g * V7X_LANES:(g + 1) * V7X_LANES]
    shift = heads
    while shift < V7X_LANES:
        acc = acc + pltpu.roll(acc, shift, 1)
        shift *= 2
    return jnp.concatenate([acc] * groups, axis=1)


def _lower_bound_body(logits_ref, lb_ref):
    x = logits_ref[...]
    e = jnp.exp(x - jnp.max(x, axis=0, keepdims=True))
    probs = e / jnp.sum(e, axis=0, keepdims=True)
    acc = jnp.zeros_like(probs[0:1])
    lb_ref[0:1, :] = acc
    for l in range(1, x.shape[0]):
        acc = acc + probs[l:l + 1]
        lb_ref[l:l + 1, :] = acc


def _lower_bounds(logits):
    return pl.pallas_call(
        _lower_bound_body,
        out_shape=jax.ShapeDtypeStruct(logits.shape, F32),
        name="hgrn_lower_bounds",
    )(logits)


def _in_proj_body(x_ref, g_ref, w_ref, o_ref, h_ref):
    @pl.when(pl.program_id(1) == 0)
    def _():
        x = x_ref[...]
        ms = jnp.mean(x * x, axis=-1, keepdims=True)
        h_ref[...] = (x * lax.rsqrt(ms + NORM_EPS) * g_ref[...]).astype(BF16)

    o_ref[...] = jnp.dot(h_ref[...], w_ref[...], preferred_element_type=F32)


def _in_proj(x, g, w, layer, tm, tn, name):
    t, d = x.shape
    n = w.shape[2]
    return pl.pallas_call(
        _in_proj_body,
        grid=(t // tm, n // tn),
        in_specs=[
            pl.BlockSpec((tm, d), lambda i, j: (i, 0)),
            pl.BlockSpec((1, d), lambda i, j: (0, 0)),
            pl.BlockSpec((None, d, tn), lambda i, j: (layer, 0, j)),
        ],
        out_specs=pl.BlockSpec((tm, tn), lambda i, j: (i, j)),
        out_shape=jax.ShapeDtypeStruct((t, n), F32),
        scratch_shapes=[pltpu.VMEM((tm, d), BF16)],
        compiler_params=_params("parallel", "arbitrary"),
        name=name,
    )(x, g, w)


def _rwkv_pre_body(cur_ref, tail_ref, bnd_ref, mu_ref, w0_ref, a0_ref, kk_ref, ka_ref, rk_ref,
                   lora_hi_ref, lora_lo_ref,
                   r_o, w_o, k_o, v_o, kk_o, b_o, bonus_o,
                   *, prompt_tiles, tiles_per_seq, sample_len, heads):
    wa_ = r_o.shape[1]
    cur = cur_ref[...]
    i = pl.program_id(0)
    row = lax.broadcasted_iota(jnp.int32, (cur.shape[0], 1), 0)
    carry = jnp.where(i % tiles_per_seq == 0, 0.0, tail_ref[ROW_TILE - 1:ROW_TILE, :])
    prev = jnp.where(row == 0, carry, pltpu.roll(cur, 1, 0))
    seq_start = jnp.logical_and(i >= prompt_tiles, jnp.bitwise_and(row, sample_len - 1) == 0)
    prev = jnp.where(seq_start, bnd_ref[...], prev)
    sh = cur + (prev - cur) * mu_ref[...]
    r = sh[:, :wa_]
    k = sh[:, wa_:2 * wa_]
    v = sh[:, 2 * wa_:3 * wa_]
    wa = sh[:, 3 * wa_:3 * wa_ + 2 * LORA]
    lane = lax.broadcasted_iota(jnp.int32, wa.shape, 1)
    t = jnp.where(lane < LORA, jnp.tanh(wa), wa)
    t1 = t.astype(BF16)
    t2 = (t - t1.astype(F32)).astype(BF16)
    bh = lora_hi_ref[...]
    bl = lora_lo_ref[...]
    lora = (jnp.dot(t1, bl, preferred_element_type=F32)
            + jnp.dot(t2, bh, preferred_element_type=F32)
            + jnp.dot(t1, bh, preferred_element_type=F32))
    z = -(w0_ref[...] + lora[:, :wa_])
    softplus = jnp.maximum(z, 0.0) + jnp.log1p(jnp.exp(-jnp.abs(z)))
    log_w = -jnp.exp(-softplus - 0.5)
    a = _sigmoid(a0_ref[...] + lora[:, wa_:])
    kk = k * kk_ref[...]
    kk = kk / jnp.maximum(jnp.sqrt(_head_sum(kk * kk, heads)), KK_NORM_FLOOR)
    k2 = k * (1.0 + (a - 1.0) * ka_ref[...])
    pos = jnp.bitwise_and(row, jnp.where(i >= prompt_tiles, sample_len - 1, REC_BLOCK - 1))
    cl = log_w
    shift = 1
    while shift < max(REC_BLOCK, sample_len):
        cl = cl + jnp.where(pos >= shift, pltpu.roll(cl, shift, 0), 0.0)
        shift *= 2
    grow = jnp.exp(-cl)
    decayed = jnp.exp(cl)
    r_o[...] = r * decayed
    w_o[...] = decayed
    k_o[...] = k2 * grow
    v_o[...] = v
    kk_o[...] = kk * jnp.exp(cl - log_w)
    b_o[...] = kk * a * grow
    bonus_o[...] = _head_sum(r * k2 * rk_ref[...], heads) * v


def _rwkv_pre(cur, bnd, mu, w0, a0, k_k, k_a, r_k, lora_hi, lora_lo, layer, *, tm, prompt_rows, seq, sample_len):
    t, sw = cur.shape
    wa_ = w0.shape[1]
    assert prompt_rows % tm == 0 and seq % tm == 0 and tm % sample_len == 0 and tm % ROW_TILE == 0
    assert sample_len & (sample_len - 1) == 0 and tm % REC_BLOCK == 0
    prompt_tiles = prompt_rows // tm
    row = lambda i: (i, 0)
    fixed = lambda i: (0, 0)
    vec = pl.BlockSpec((1, wa_), fixed)
    lora_spec = pl.BlockSpec((None,) + lora_hi.shape[1:], lambda i: (layer, 0, 0))
    out = jax.ShapeDtypeStruct((t, wa_), F32)
    return pl.pallas_call(
        functools.partial(_rwkv_pre_body, prompt_tiles=prompt_tiles, tiles_per_seq=seq // tm,
                          sample_len=sample_len, heads=wa_ // HEAD_A),
        grid=(t // tm,),
        in_specs=[
            pl.BlockSpec((tm, sw), row),
            pl.BlockSpec((ROW_TILE, sw), lambda i: (jnp.maximum(i * (tm // ROW_TILE) - 1, 0), 0)),
            pl.BlockSpec((tm, sw), lambda i: (jnp.maximum(i - prompt_tiles, 0), 0)),
            pl.BlockSpec((1, sw), fixed),
            vec, vec, vec, vec, vec,
            lora_spec, lora_spec,
        ],
        out_specs=[pl.BlockSpec((tm, wa_), row)] * 7,
        out_shape=[out] * 7,
        compiler_params=_params("parallel"),
        name="rwkv_pre",
    )(cur, cur, bnd, mu, w0, a0, k_k, k_a, r_k, lora_hi, lora_lo)


def _rows_to_lanes(x_refs, z_ref):
    for n, x_ref in enumerate(x_refs):
        for c in range(x_ref.shape[1] // V7X_LANES):
            cols = slice(c * V7X_LANES, (c + 1) * V7X_LANES)
            z_ref[n, cols, :] = x_ref[:, cols].T


def _to_lanes_key_body(*refs, n_seq, heads, dup):
    x_refs, o_ref, z_ref = refs[:n_seq], refs[n_seq], refs[n_seq + 1]
    _rows_to_lanes(x_refs, z_ref)
    for k in range(o_ref.shape[0]):
        rows = slice(k * heads, (k + 1) * heads)
        a = jnp.concatenate([z_ref[n, rows, :] for n in range(n_seq)] * dup, axis=0)
        o_ref[k] = a.T


def _to_lanes_val_body(*refs, n_seq, heads, dup):
    x_refs, o_ref, z_ref = refs[:n_seq], refs[n_seq], refs[n_seq + 1]
    _rows_to_lanes(x_refs, z_ref)
    vl = HEAD_A // dup
    for v in range(vl):
        a = jnp.concatenate([z_ref[n, (v * dup + vh) * heads:(v * dup + vh + 1) * heads, :]
                             for vh in range(dup) for n in range(n_seq)], axis=0)
        o_ref[pl.ds(v, T_BLOCK, stride=vl), :] = a.T


def _from_lanes_val_body(y_ref, o_ref, z_ref, *, n_seq, heads, dup):
    vl = HEAD_A // dup
    for v in range(vl):
        a = y_ref[pl.ds(v, T_BLOCK, stride=vl), :].T
        i = 0
        for vh in range(dup):
            for n in range(n_seq):
                z_ref[n, (v * dup + vh) * heads:(v * dup + vh + 1) * heads, :] = a[i * heads:(i + 1) * heads, :]
                i += 1
    for n in range(n_seq):
        for c in range(o_ref.shape[2] // V7X_LANES):
            cols = slice(c * V7X_LANES, (c + 1) * V7X_LANES)
            o_ref[n, :, cols] = z_ref[n, cols, :].T


def _seq_specs(n_seq, length, width):
    blocks = length // T_BLOCK
    return [pl.BlockSpec((T_BLOCK, width), lambda j, n=n: (n * blocks + j, 0)) for n in range(n_seq)]


def _to_lanes_key(x, n_seq, length, heads, dup):
    width = heads * HEAD_A
    assert n_seq * heads * dup == V7X_LANES and length % T_BLOCK == 0
    return pl.pallas_call(
        functools.partial(_to_lanes_key_body, n_seq=n_seq, heads=heads, dup=dup),
        grid=(length // T_BLOCK,),
        in_specs=_seq_specs(n_seq, length, width),
        out_specs=pl.BlockSpec((HEAD_A, T_BLOCK, V7X_LANES), lambda j: (0, j, 0)),
        out_shape=jax.ShapeDtypeStruct((HEAD_A, length, V7X_LANES), F32),
        scratch_shapes=[pltpu.VMEM((n_seq, width, T_BLOCK), F32)],
        compiler_params=_params("parallel"),
        name="to_lanes_key",
    )(*([x] * n_seq))


def _to_lanes_val(x, n_seq, length, heads, dup):
    width = heads * HEAD_A
    vl = HEAD_A // dup
    assert n_seq * heads * dup == V7X_LANES and length % T_BLOCK == 0
    out = pl.pallas_call(
        functools.partial(_to_lanes_val_body, n_seq=n_seq, heads=heads, dup=dup),
        grid=(length // T_BLOCK,),
        in_specs=_seq_specs(n_seq, length, width),
        out_specs=pl.BlockSpec((T_BLOCK * vl, V7X_LANES), lambda j: (j, 0)),
        out_shape=jax.ShapeDtypeStruct((length * vl, V7X_LANES), F32),
        scratch_shapes=[pltpu.VMEM((n_seq, width, T_BLOCK), F32)],
        compiler_params=_params("parallel"),
        name="to_lanes_val",
    )(*([x] * n_seq))
    return out.reshape(length, vl, V7X_LANES)


def _from_lanes_val(y, n_seq, heads, dup):
    length, vl, _ = y.shape
    width = heads * HEAD_A
    out = pl.pallas_call(
        functools.partial(_from_lanes_val_body, n_seq=n_seq, heads=heads, dup=dup),
        grid=(length // T_BLOCK,),
        in_specs=[pl.BlockSpec((T_BLOCK * vl, V7X_LANES), lambda j: (j, 0))],
        out_specs=pl.BlockSpec((n_seq, T_BLOCK, width), lambda j: (0, j, 0)),
        out_shape=jax.ShapeDtypeStruct((n_seq, length, width), F32),
        scratch_shapes=[pltpu.VMEM((n_seq, width, T_BLOCK), F32)],
        compiler_params=_params("parallel"),
        name="from_lanes_val",
    )(y.reshape(length * vl, V7X_LANES))
    return out.reshape(n_seq * length, width)


def _rwkv_rec_body(kk_ref, w_ref, b_ref, k_ref, r_ref, v_ref, s0_ref, y_ref, s_ref, *, key_major):
    n_key = s_ref.shape[0]

    def key_row(ref, k, t):
        return ref[k, pl.ds(t, 1), :] if key_major else ref[t, k:k + 1, :]

    @pl.when(pl.program_id(1) == 0)
    def _():
        s_ref[...] = s0_ref[...]

    def tree(parts):
        while len(parts) > 1:
            parts = [parts[i] + parts[i + 1] for i in range(0, len(parts), 2)]
        return parts[0]

    def step(t):
        acc = [None] * 4
        for k in range(n_key):
            term = s_ref[k] * key_row(kk_ref, k, t)
            acc[k % 4] = term if acc[k % 4] is None else acc[k % 4] + term
        sa = tree(acc)
        vt = v_ref[t]
        acc = [None] * 4
        for k in range(n_key):
            s_new = s_ref[k] + (vt * key_row(k_ref, k, t) - sa * key_row(b_ref, k, t))
            s_ref[k] = s_new
            term = s_new * key_row(r_ref, k, t)
            acc[k % 4] = term if acc[k % 4] is None else acc[k % 4] + term
        y_ref[t] = tree(acc)

    def body(it, carry):
        for u in range(RWKV_STEPS_PER_ITER):
            step(it * RWKV_STEPS_PER_ITER + u)
        return carry

    lax.fori_loop(0, v_ref.shape[0] // RWKV_STEPS_PER_ITER, body, 0)
    last = pl.program_id(1) if key_major else v_ref.shape[0] - 1
    for k in range(n_key):
        s_ref[k] = s_ref[k] * key_row(w_ref, k, last)


def _rwkv_rec(kk, w, b, k, r, v, s0, tb, key_major):
    length, vl, lanes = v.shape
    n_key = s0.shape[0]
    assert tb % RWKV_STEPS_PER_ITER == 0
    if key_major:
        key_spec = pl.BlockSpec((n_key, tb, V7X_LANES), lambda g, j: (0, j, g))
        decay_spec = pl.BlockSpec((n_key, length // tb, V7X_LANES), lambda g, j: (0, 0, g))
    else:
        key_spec = pl.BlockSpec((tb, n_key, V7X_LANES), lambda g, j: (j, 0, g))
        decay_spec = key_spec
    val_spec = pl.BlockSpec((tb, vl, V7X_LANES), lambda g, j: (j, 0, g))
    st_spec = pl.BlockSpec((n_key, vl, V7X_LANES), lambda g, j: (0, 0, g))
    return pl.pallas_call(
        functools.partial(_rwkv_rec_body, key_major=key_major),
        grid=(lanes // V7X_LANES, length // tb),
        in_specs=[key_spec, decay_spec, key_spec, key_spec, key_spec, val_spec, st_spec],
        out_specs=[val_spec, st_spec],
        out_shape=[jax.ShapeDtypeStruct(v.shape, F32), jax.ShapeDtypeStruct(s0.shape, F32)],
        compiler_params=_params("parallel", "arbitrary"),
        name="rwkv_rec",
    )(kk, w, b, k, r, v, s0)


def _cumsum_rows(x):
    n = x.shape[0]
    row = lax.broadcasted_iota(jnp.int32, x.shape, 0)
    sh = 1
    while sh < n:
        x = x + jnp.where(row >= sh, pltpu.roll(x, sh, 0), 0.0)
        sh *= 2
    return x


def _mm(a, b, dims):
    return lax.dot_general(a.astype(BF16), b.astype(BF16), (dims, ((), ())),
                           preferred_element_type=F32)


def _mm_state(a, st, dims):
    a1 = a.astype(BF16)
    s1 = st.astype(BF16)
    s2 = (st - s1.astype(F32)).astype(BF16)
    dn = (dims, ((), ()))
    dot = functools.partial(lax.dot_general, dimension_numbers=dn, preferred_element_type=F32)
    return dot(a1, s2) + dot(a1, s1)


def _gla_chunk(qp, f, iv, lb, st, sub):
    c = qp.shape[0]
    q = qp * _sigmoid(qp)
    kb = jnp.minimum((1.0 - lb) * _sigmoid(-f), MAX_INPUT_GATE)
    b = _cumsum_rows(jnp.log1p(-kb))
    o_inter = _mm_state(q * jnp.exp(b), st, ((1,), (1,)))
    outs = []
    for i in range(c // sub):
        lo = i * sub
        qi = q[lo:lo + sub]
        bi = b[lo:lo + sub]
        oi = o_inter[lo:lo + sub]
        tiles = []
        for r0 in range(0, sub, ROW_TILE):
            rt = min(ROW_TILE, sub - r0)
            q_t, b_t, o_t = qi[r0:r0 + rt], bi[r0:r0 + rt], oi[r0:r0 + rt]
            row = lax.broadcasted_iota(jnp.int32, (rt, 1), 0) + r0
            for s in range(min(sub, r0 + rt)):
                bs = b[lo + s:lo + s + 1]
                e = jnp.exp(jnp.minimum(b_t - bs, 0.0))
                att = jnp.sum(q_t * e * kb[lo + s:lo + s + 1], axis=-1, keepdims=True)
                if s > r0:
                    att = jnp.where(row >= s, att, 0.0)
                o_t = o_t + att * iv[lo + s:lo + s + 1]
            tiles.append(o_t)
        oi = tiles[0] if len(tiles) == 1 else jnp.concatenate(tiles, axis=0)
        if i > 0:
            b_edge = b[lo - 1:lo]
            qt = qi * jnp.exp(bi - b_edge)
            kh = kb[:lo] * jnp.exp(b_edge - b[:lo])
            att = _mm(qt, kh, ((1,), (1,)))
            oi = oi + _mm(att, iv[:lo], ((1,), (0,)))
        outs.append(oi)
    b_last = b[c - 1:c]
    kt = kb * jnp.exp(b_last - b)
    st_new = st * jnp.exp(b_last) + _mm(iv, kt, ((0,), (0,)))
    return jnp.concatenate(outs, axis=0), st_new


def _hgrn_body(q_ref, f_ref, i_ref, lb_ref, s0_ref, o_ref, s_ref, st_ref, *, n_seq, chunk, sub):
    n_heads = q_ref.shape[1] // HEAD_B
    pairs = [(sq, h) for h in range(n_heads) for sq in range(n_seq)]

    @pl.when(pl.program_id(1) == 0)
    def _():
        for sq, h in pairs:
            st_ref[sq, h] = s0_ref[sq, h].T

    for sq, h in pairs:
        col = slice(h * HEAD_B, (h + 1) * HEAD_B)
        rows = slice(sq * chunk, (sq + 1) * chunk)
        o, st = _gla_chunk(q_ref[rows, col], f_ref[rows, col], i_ref[rows, col], lb_ref[:, col],
                           st_ref[sq, h], sub)
        o_ref[rows, col] = o
        st_ref[sq, h] = st

    @pl.when(pl.program_id(1) == pl.num_programs(1) - 1)
    def _():
        for sq, h in pairs:
            s_ref[sq, h] = st_ref[sq, h].T


def _hgrn(pb, lb, s0, layer, *, row_off, n_seq_total, length, chunk, n_seq, col_q, col_f, col_i):
    wb = lb.shape[1]
    n_heads = wb // HEAD_B
    rows = n_seq * chunk
    n_chunks = length // chunk
    assert n_seq == 1 or n_chunks == 1
    assert row_off % rows == 0 and n_seq_total % n_seq == 0
    base = row_off // rows

    def tok(cb):
        return pl.BlockSpec((rows, wb), lambda g, c: (base + g * n_chunks + c, cb))

    st_block = (n_seq, n_heads, HEAD_B, HEAD_B)
    return pl.pallas_call(
        functools.partial(_hgrn_body, n_seq=n_seq, chunk=chunk, sub=min(GLA_SUB, chunk)),
        grid=(n_seq_total // n_seq, n_chunks),
        in_specs=[tok(col_q), tok(col_f), tok(col_i), pl.BlockSpec((1, wb), lambda g, c: (0, 0)),
                  pl.BlockSpec((None,) + st_block, lambda g, c: (layer, g, 0, 0, 0))],
        out_specs=[pl.BlockSpec((rows, wb), lambda g, c: (g * n_chunks + c, 0)),
                   pl.BlockSpec(st_block, lambda g, c: (g, 0, 0, 0))],
        out_shape=[jax.ShapeDtypeStruct((n_seq_total * length, wb), F32),
                   jax.ShapeDtypeStruct(s0.shape[1:], F32)],
        scratch_shapes=[pltpu.VMEM(st_block, F32)],
        compiler_params=_params("parallel", "arbitrary"),
        name="hgrn",
    )(pb, pb, pb, lb, s0)


def _out_proj_body(x_ref, yrec_p_ref, yrec_s_ref, bonus_ref, za_ref, ob_p_ref, ob_s_ref, zb_ref, ga_ref, gb_ref,
                   lnw_ref, lnb_ref, hg_ref, pa_ref, pb_ref, wo_ref, y_ref, *, prompt_tiles):
    heads = yrec_p_ref.shape[1] // HEAD_A
    inv_head = 1.0 / HEAD_A
    is_prompt = pl.program_id(0) < prompt_tiles
    y = jnp.where(is_prompt, yrec_p_ref[...], yrec_s_ref[...])
    mu = _head_sum(y, heads) * inv_head
    yc = y - mu
    var = _head_sum(yc * yc, heads) * inv_head
    za = za_ref[...]
    ya = (yc * lax.rsqrt(var + GN_EPS) * lnw_ref[...] + lnb_ref[...] + bonus_ref[...]) * (za * _sigmoid(za))
    ob = jnp.where(is_prompt, ob_p_ref[...], ob_s_ref[...])
    zb = zb_ref[...]
    hg = hg_ref[...]
    parts = []
    for g in range(ob.shape[1] // HEAD_B):
        og = ob[:, g * HEAD_B:(g + 1) * HEAD_B]
        ms = jnp.mean(og * og, axis=-1, keepdims=True)
        parts.append(og * lax.rsqrt(ms + NORM_EPS) * hg)
    yb = jnp.concatenate(parts, axis=1) * (zb * _sigmoid(zb))
    merged = (_sigmoid(ga_ref[...]) * jnp.dot(ya.astype(BF16), pa_ref[...], preferred_element_type=F32)
              + _sigmoid(gb_ref[...]) * jnp.dot(yb.astype(BF16), pb_ref[...], preferred_element_type=F32))
    y_ref[...] = x_ref[...] + jnp.dot(merged.astype(BF16), wo_ref[...], preferred_element_type=F32)


def _out_proj(x, yrec_p, yrec_s, bonus, pb, ob_p, ob_s, ln_w, ln_b, hg_g, proj_a, proj_b, w_out, layer,
              *, tm, col_za, col_zb, col_ga, col_gb):
    t, d = x.shape
    wa_ = yrec_p.shape[1]
    assert yrec_p.shape[0] % tm == 0 and yrec_s.shape[0] % tm == 0
    prompt_tiles = yrec_p.shape[0] // tm
    row = lambda i: (i, 0)
    fixed = lambda i: (0, 0)
    half = lambda cb: pl.BlockSpec((tm, wa_), lambda i: (i, cb))
    full = lambda cb: pl.BlockSpec((tm, d), lambda i: (i, cb))
    first = pl.BlockSpec((tm, wa_), lambda i: (jnp.minimum(i, prompt_tiles - 1), 0))
    second = pl.BlockSpec((tm, wa_), lambda i: (jnp.maximum(i - prompt_tiles, 0), 0))
    const = lambda a: pl.BlockSpec(a.shape, fixed, pipeline_mode=pl.Buffered(1))
    weight = lambda a: pl.BlockSpec((None,) + a.shape[1:], lambda i: (layer, 0, 0),
                                    pipeline_mode=pl.Buffered(1))
    return pl.pallas_call(
        functools.partial(_out_proj_body, prompt_tiles=prompt_tiles),
        grid=(t // tm,),
        in_specs=[
            pl.BlockSpec((tm, d), row), first, second, half(0), half(col_za), first, second, half(col_zb),
            full(col_ga), full(col_gb),
            const(ln_w), const(ln_b), const(hg_g),
            weight(proj_a), weight(proj_b), weight(w_out),
        ],
        out_specs=pl.BlockSpec((tm, d), row),
        out_shape=jax.ShapeDtypeStruct((t, d), F32),
        compiler_params=_params("parallel"),
        name="out_proj",
    )(x, yrec_p, yrec_s, bonus, pb, ob_p, ob_s, pb, pb, pb, ln_w, ln_b, hg_g, proj_a, proj_b, w_out)


def _final_norm_body(x_ref, g_ref, o_ref):
    x = x_ref[...]
    ms = jnp.mean(x * x, axis=-1, keepdims=True)
    o_ref[...] = x * lax.rsqrt(ms + NORM_EPS) * g_ref[...]


def _final_norm(x, g, row_off, rows, tm):
    d = x.shape[1]
    assert row_off % tm == 0 and rows % tm == 0
    first = row_off // tm
    return pl.pallas_call(
        _final_norm_body,
        grid=(rows // tm,),
        in_specs=[pl.BlockSpec((tm, d), lambda i: (first + i, 0)), pl.BlockSpec((1, d), lambda i: (0, 0))],
        out_specs=pl.BlockSpec((tm, d), lambda i: (i, 0)),
        out_shape=jax.ShapeDtypeStruct((rows, d), F32),
        compiler_params=_params("parallel"),
        name="final_norm",
    )(x, g)


def _perm_key(a, heads):
    lead = a.shape[:-1]
    return jnp.swapaxes(a.reshape(*lead, heads, HEAD_A), -1, -2).reshape(*lead, heads * HEAD_A)


def _unperm_key(a, heads):
    lead = a.shape[:-1]
    return jnp.swapaxes(a.reshape(*lead, HEAD_A, heads), -1, -2).reshape(*lead, heads * HEAD_A)


def _perm_val(a, heads, dup):
    lead = a.shape[:-1]
    x = a.reshape(*lead, heads, dup, HEAD_A // dup)
    return jnp.moveaxis(x, (-3, -2, -1), (-1, -2, -3)).reshape(*lead, heads * HEAD_A)


def _unperm_val(a, heads, dup):
    lead = a.shape[:-1]
    x = a.reshape(*lead, HEAD_A // dup, dup, heads)
    return jnp.moveaxis(x, (-3, -2, -1), (-1, -2, -3)).reshape(*lead, heads * HEAD_A)


def _perm_shift(a, heads, dup):
    w = heads * HEAD_A
    return jnp.concatenate([_perm_key(a[..., :w], heads), _perm_key(a[..., w:2 * w], heads),
                            _perm_val(a[..., 2 * w:3 * w], heads, dup), a[..., 3 * w:]], axis=-1)


def _unperm_shift(a, heads, dup):
    w = heads * HEAD_A
    return jnp.concatenate([_unperm_key(a[..., :w], heads), _unperm_key(a[..., w:2 * w], heads),
                            _unperm_val(a[..., 2 * w:3 * w], heads, dup), a[..., 3 * w:]], axis=-1)


def kernel(x_prompt, x_sample, state_rwkv, state_hgrn, state_shift, norm_g, w_in, shift_mu, rwkv_w0, rwkv_w2, rwkv_a0, rwkv_a2, rwkv_k_k, rwkv_k_a, rwkv_r_k, rwkv_ln_w, rwkv_ln_b, hgrn_lb_logits, hgrn_norm_g, proj_a, proj_b, w_out, final_norm_g):
    nb, seq, d = x_prompt.shape
    db, dseq, _ = x_sample.shape
    depth = w_in.shape[0]
    wa_ = rwkv_w0.shape[1]
    wb_ = hgrn_lb_logits.shape[1]
    heads_a = wa_ // HEAD_A
    heads_b = wb_ // HEAD_B
    sw = 3 * wa_ + 2 * LORA
    assert w_in.shape[2] == sw + wa_ + 4 * wb_ + 2 * d and wa_ == wb_ and d == 2 * wa_
    dup = V7X_LANES // (nb * heads_a)
    assert nb * heads_a * dup == V7X_LANES and (db * heads_a) % V7X_LANES == 0
    vl = HEAD_A // dup
    tp, ts = nb * seq, db * dseq
    t_all = tp + ts
    pk = functools.partial(_perm_key, heads=heads_a)
    pv = functools.partial(_perm_val, heads=heads_a, dup=dup)

    off = sw
    za_w = pv(w_in[:, :, off:off + wa_])
    qb_w = w_in[:, :, off + wa_:off + wa_ + wb_]
    fb_w = w_in[:, :, off + wa_ + wb_:off + wa_ + 2 * wb_]
    ib_w = w_in[:, :, off + wa_ + 2 * wb_:off + wa_ + 3 * wb_]
    zb_w = w_in[:, :, off + wa_ + 3 * wb_:off + wa_ + 4 * wb_]
    ga_w = w_in[:, :, off + wa_ + 4 * wb_:off + wa_ + 4 * wb_ + d]
    gb_w = w_in[:, :, off + wa_ + 4 * wb_ + d:]
    w_first = _perm_shift(w_in[:, :, :sw], heads_a, dup).astype(BF16)
    w_second = jnp.concatenate([ga_w, gb_w, za_w, qb_w, fb_w, ib_w, zb_w], axis=2).astype(BF16)
    col_ga, col_gb = 0, 1
    col_za, col_q, col_f, col_i, col_zb = 4, 5, 6, 7, 8
    proj_a_h = (proj_a.reshape(depth, heads_a, dup, vl, d).transpose(0, 3, 2, 1, 4)
                .reshape(depth, wa_, d).astype(BF16))
    proj_b_h, w_out_h = proj_b.astype(BF16), w_out.astype(BF16)
    mu_p = _perm_shift(shift_mu, heads_a, dup)
    w0_p, a0_p, kk_p, ka_p = pk(rwkv_w0), pk(rwkv_a0), pk(rwkv_k_k), pk(rwkv_k_a)
    rk_p = pk(rwkv_r_k.reshape(depth, wa_))
    lnw_p, lnb_p = pv(rwkv_ln_w), pv(rwkv_ln_b)

    zero = jnp.zeros((depth, LORA, wa_), F32)
    lora = jnp.concatenate([jnp.concatenate([pk(rwkv_w2), zero], axis=2),
                            jnp.concatenate([zero, pk(rwkv_a2)], axis=2)], axis=1)
    lora_hi = lora.astype(BF16)
    lora_lo = (lora - lora_hi.astype(F32)).astype(BF16)

    lbs = _lower_bounds(hgrn_lb_logits)

    tm_in = _tile(t_all, 1024)
    tm_pre = _tile(math.gcd(seq, ts), 256)
    tm_out = _tile(math.gcd(tp, ts), 256)
    assert seq % REC_BLOCK == 0 and dseq <= REC_BLOCK
    tb_p = REC_BLOCK
    chunk_p = _tile(seq, 64)
    ns_s = _tile(db, 2, 1)

    x = jnp.concatenate([x_prompt.reshape(tp, d), x_sample.reshape(ts, d)], axis=0)
    zero_a = jnp.zeros((HEAD_A, vl, V7X_LANES), F32)
    zero_b = jnp.zeros((1, nb, heads_b, HEAD_B, HEAD_B), F32)
    shift_in = jnp.pad(_perm_shift(state_shift, heads_a, dup)[:, :, None, :],
                       ((0, 0), (0, 0), (0, dseq - 1), (0, 0))).reshape(depth, ts, sw)
    lanes_s = db * heads_a
    outs = [[] for _ in range(6)]
    for l in range(depth):
        row = lambda a: a[l][None]
        p1 = _in_proj(x, row(norm_g), w_first, l, tm_in, _tile(sw, 640, V7X_LANES), "in_proj_shift")
        p2 = _in_proj(x, row(norm_g), w_second, l, tm_in, _tile(w_second.shape[2], 1024, V7X_LANES), "in_proj_rest")

        r, w, k, v, kk, b, bonus = _rwkv_pre(
            p1, shift_in[l], row(mu_p), row(w0_p), row(a0_p), row(kk_p), row(ka_p), row(rk_p),
            lora_hi, lora_lo, l, tm=tm_pre, prompt_rows=tp, seq=seq, sample_len=dseq)

        kk_l, b_l, k_l, r_l = [_to_lanes_key(a, nb, seq, heads_a, dup) for a in (kk, b, k, r)]
        w_l = jnp.tile(w[REC_BLOCK - 1:tp:REC_BLOCK].reshape(nb, seq // REC_BLOCK, HEAD_A, heads_a)
                       .transpose(2, 1, 0, 3).reshape(HEAD_A, seq // REC_BLOCK, nb * heads_a), (1, 1, dup))
        y_p, sa_p = _rwkv_rec(kk_l, w_l, b_l, k_l, r_l, _to_lanes_val(v, nb, seq, heads_a, dup), zero_a, tb_p, True)
        y_p = _from_lanes_val(y_p, nb, heads_a, dup)
        sa_p = (sa_p.reshape(HEAD_A, vl, dup, nb, heads_a).transpose(3, 4, 2, 1, 0)
                .reshape(nb, heads_a, HEAD_A, HEAD_A))
        to_s = lambda a: a[tp:].reshape(db, dseq * wa_).T.reshape(dseq, HEAD_A, lanes_s)
        s0_s = (state_rwkv[l].reshape(db, heads_a, dup, vl, HEAD_A).transpose(4, 3, 2, 1, 0)
                .reshape(HEAD_A, HEAD_A, lanes_s))
        y_s, sa_s = _rwkv_rec(*[to_s(a) for a in (kk, w, b, k, r, v)], s0_s, dseq, False)
        y_s = y_s.reshape(dseq * wa_, db).T.reshape(ts, wa_)
        sa_s = (sa_s.reshape(HEAD_A, vl, dup, heads_a, db).transpose(4, 3, 2, 1, 0)
                .reshape(db, heads_a, HEAD_A, HEAD_A))
        cols = dict(col_q=col_q, col_f=col_f, col_i=col_i)
        o_p, sb_p = _hgrn(p2, row(lbs), zero_b, 0, row_off=0, n_seq_total=nb, length=seq,
                          chunk=chunk_p, n_seq=1, **cols)
        o_s, sb_s = _hgrn(p2, row(lbs), state_hgrn, l, row_off=tp, n_seq_total=db,
                          length=dseq, chunk=dseq, n_seq=ns_s, **cols)
        x = _out_proj(x, y_p, y_s, bonus, p2, o_p, o_s, row(lnw_p), row(lnb_p), row(hgrn_norm_g),
                      proj_a_h, proj_b_h, w_out_h, l,
                      tm=tm_out, col_za=col_za, col_zb=col_zb, col_ga=col_ga, col_gb=col_gb)

        last_p = _unperm_shift(p1[seq - 1:tp:seq], heads_a, dup)
        last_s = _unperm_shift(p1[tp + dseq - 1::dseq], heads_a, dup)
        for dst, val in zip(outs, (sa_p, sb_p, last_p, sa_s, sb_s, last_s)):
            dst.append(val)

    tm_fin = _tile(math.gcd(tp, ts), 512)
    y_prompt = _final_norm(x, final_norm_g[None], 0, tp, tm_fin).reshape(nb, seq, d)
    y_sample = _final_norm(x, final_norm_g[None], tp, ts, tm_fin).reshape(db, dseq, d)
    return (y_prompt, y_sample) + tuple(jnp.stack(o) for o in outs)
```
